```python
import math
import jax, jax.numpy as jnp
from jax import lax
import numpy as np

D_MODEL = 1024
BATCH = 16
SEQ = 2048
DEPTH = 4
DEC_BATCH = 1
DEC_SEQ = 16384
PAST_LEN = 128

GRID_W = 64
HEAD_DIM = 64
NA_HEADS = 4
NA_WIN_ROWS = 8
NA_WIN_COLS = 16
GQ_HEADS = 8
GQ_KV_HEADS = 2
CONV_CH = 256
CONV_WIDTH = 31
D_FF = 2816
ROPE_THETA = 10000.0
Q_BLOCK = 128
EPS = 1e-6
NEG_INF = -1e30
NA_W = NA_HEADS * HEAD_DIM
GQ_W = GQ_HEADS * HEAD_DIM
KV_W = GQ_KV_HEADS * HEAD_DIM
MIX_W = NA_W + GQ_W + CONV_CH
IN_COLS = 3 * NA_W + GQ_W + 2 * KV_W + 2 * CONV_CH
N_MOD = 9
RPB_ROWS = 2 * NA_WIN_ROWS - 1
RPB_COLS = 2 * NA_WIN_COLS - 1

kernel_name = "hybrid_natten_gqa_conformer_encoder"


def rms_norm(x, g):
    xf = x.astype(jnp.float32)
    y = xf * lax.rsqrt(jnp.mean(xf * xf, axis=-1, keepdims=True) + EPS)
    return (y * g.astype(jnp.float32)).astype(x.dtype)


def modulate(h, shift, scale):
    return h * (1 + scale[:, None, :]) + shift[:, None, :]


def swiglu(h, w_gu, w_down):
    gate, up = jnp.split(h @ w_gu, 2, axis=-1)
    return (jax.nn.silu(gate) * up) @ w_down


def rope_2d_angles(L):
    t = jnp.arange(L)
    row = (t // GRID_W).astype(jnp.float32)
    col = (t % GRID_W).astype(jnp.float32)
    half = HEAD_DIM // 2
    freqs = ROPE_THETA ** (-jnp.arange(0, half, 2, dtype=jnp.float32) / half)
    return jnp.concatenate([row[:, None] * freqs, col[:, None] * freqs], axis=-1)


def apply_rope(x, cos, sin):
    B, L, H, dh = x.shape
    xp = x.reshape(B, L, H, dh // 2, 2)
    x1, x2 = xp[..., 0], xp[..., 1]
    c = cos[None, :, None, :]
    s = sin[None, :, None, :]
    out = jnp.stack([x1 * c - x2 * s, x1 * s + x2 * c], axis=-1)
    return out.reshape(B, L, H, dh)


def neighbourhood_attention(q, k, v, rpb):
    B, L, H, dh = q.shape
    rows = L // GRID_W
    kr = min(NA_WIN_ROWS, rows)
    kc = NA_WIN_COLS
    r = jnp.arange(rows)
    c = jnp.arange(GRID_W)
    rs = jnp.clip(r - kr // 2, 0, rows - kr)
    cs = jnp.clip(c - kc // 2, 0, GRID_W - kc)
    row_idx = rs[:, None] + jnp.arange(kr)[None, :]
    qg = q.reshape(B, rows, GRID_W, H, dh)
    kg = k.reshape(B, rows, GRID_W, H, dh)[:, row_idx]
    vg = v.reshape(B, rows, GRID_W, H, dh)[:, row_idx]
    s = jnp.einsum('brqhd,brjwhd->brhqjw', qg, kg, preferred_element_type=jnp.float32) * (dh ** -0.5)
    row_off = row_idx - r[:, None] + (NA_WIN_ROWS - 1)
    col_off = jnp.clip(c[None, :] - c[:, None] + (NA_WIN_COLS - 1), 0, RPB_COLS - 1)
    col_ok = (c[None, :] >= cs[:, None]) & (c[None, :] < cs[:, None] + kc)
    bias = rpb[:, row_off[:, None, :, None], col_off[None, :, None, :]]
    bias = bias.transpose(1, 0, 2, 3, 4).astype(jnp.float32)
    s = jnp.where(col_ok[:, None, :], s + bias[None], NEG_INF)
    p = jax.nn.softmax(s.reshape(B, rows, H, GRID_W, kr * GRID_W), axis=-1)
    p = p.reshape(B, rows, H, GRID_W, kr, GRID_W).astype(v.dtype)
    o = jnp.einsum('brhqjw,brjwhd->brqhd', p, vg)
    return o.reshape(B, L, H * dh)


def gqa_attention(q, k, v):
    B, L, Hq, dh = q.shape
    G = Hq // GQ_KV_HEADS
    nb = L // Q_BLOCK
    qb = q.reshape(B, nb, Q_BLOCK, GQ_KV_HEADS, G, dh).transpose(1, 0, 2, 3, 4, 5)
    scale = dh ** -0.5

    def block(qi):
        s = jnp.einsum('bqkgd,bskd->bkgqs', qi, k, preferred_element_type=jnp.float32) * scale
        p = jax.nn.softmax(s, axis=-1).astype(v.dtype)
        return jnp.einsum('bkgqs,bskd->bqkgd', p, v)

    o = lax.map(block, qb)
    return o.transpose(1, 0, 2, 3, 4, 5).reshape(B, L, Hq * dh)


def conformer_conv(a, b, w_dw, b_dw, ln_g, ln_b):
    u = a * jax.nn.sigmoid(b)
    pad = CONV_WIDTH // 2
    y = lax.conv_general_dilated(u, w_dw[:, None, :], window_strides=(1,), padding=((pad, pad),),
                                 dimension_numbers=('NWC', 'WIO', 'NWC'),
                                 feature_group_count=CONV_CH) + b_dw
    yf = y.astype(jnp.float32)
    mu = jnp.mean(yf, axis=-1, keepdims=True)
    var = jnp.mean(jnp.square(yf - mu), axis=-1, keepdims=True)
    yf = (yf - mu) * lax.rsqrt(var + EPS) * ln_g.astype(jnp.float32) + ln_b.astype(jnp.float32)
    return jax.nn.silu(yf).astype(a.dtype)


def trunk(x, c, w_mod, b_mod, norm_ffn1_g, ffn1_w_gu, ffn1_w_down, norm_mix_g, w_in,
          na_q_g, na_k_g, na_rpb, gq_q_g, gq_k_g, conv_w, conv_b, conv_ln_g, conv_ln_b,
          out_norm_g, w_out, norm_ffn2_g, ffn2_w_gu, ffn2_w_down):
    B, L, D = x.shape
    ang = rope_2d_angles(L)
    cos = jnp.cos(ang).astype(x.dtype)
    sin = jnp.sin(ang).astype(x.dtype)
    c_act = jax.nn.silu(c)
    splits = [NA_W, 2 * NA_W, 3 * NA_W, 3 * NA_W + GQ_W, 3 * NA_W + GQ_W + KV_W,
              3 * NA_W + GQ_W + 2 * KV_W, 3 * NA_W + GQ_W + 2 * KV_W + CONV_CH]
    for l in range(DEPTH):
        mod = c_act @ w_mod[l] + b_mod[l]
        sh1, sc1, g1, sh2, sc2, g2, sh3, sc3, g3 = jnp.split(mod, N_MOD, axis=-1)
        h = modulate(rms_norm(x, norm_ffn1_g[l]), sh1, sc1)
        x = x + 0.5 * g1[:, None, :] * swiglu(h, ffn1_w_gu[l], ffn1_w_down[l])
        h = modulate(rms_norm(x, norm_mix_g[l]), sh2, sc2)
        z = h @ w_in[l]
        qa, ka, va, qb, kb, vb, ca, cb = jnp.split(z, splits, axis=-1)
        qa = rms_norm(qa.reshape(B, L, NA_HEADS, HEAD_DIM), na_q_g[l])
        ka = rms_norm(ka.reshape(B, L, NA_HEADS, HEAD_DIM), na_k_g[l])
        va = va.reshape(B, L, NA_HEADS, HEAD_DIM)
        ya = neighbourhood_attention(qa, ka, va, na_rpb[l])
        qb = apply_rope(rms_norm(qb.reshape(B, L, GQ_HEADS, HEAD_DIM), gq_q_g[l]), cos, sin)
        kb = apply_rope(rms_norm(kb.reshape(B, L, GQ_KV_HEADS, HEAD_DIM), gq_k_g[l]), cos, sin)
        vb = vb.reshape(B, L, GQ_KV_HEADS, HEAD_DIM)
        yb = gqa_attention(qb, kb, vb)
        yc = conformer_conv(ca, cb, conv_w[l], conv_b[l], conv_ln_g[l], conv_ln_b[l])
        og = out_norm_g[l]
        y = jnp.concatenate([rms_norm(ya, og[:NA_W]),
                             rms_norm(yb, og[NA_W:NA_W + GQ_W]),
                             rms_norm(yc, og[NA_W + GQ_W:])], axis=-1)
        x = x + g2[:, None, :] * (y @ w_out[l])
        h = modulate(rms_norm(x, norm_ffn2_g[l]), sh3, sc3)
        x = x + 0.5 * g3[:, None, :] * swiglu(h, ffn2_w_gu[l], ffn2_w_down[l])
    return x


def setup_inputs(seed: int = 0) -> dict:
    key = jax.random.key(seed)
    ks = jax.random.split(key, 32)
    f32 = jnp.float32
    D = D_MODEL

    def nrm(k, shape, scale):
        return jax.random.normal(k, shape, f32) * scale

    def gain(k, shape):
        return 1.0 + 0.05 * jax.random.normal(k, shape, f32)

    return {
        "x_prompt": nrm(ks[0], (BATCH, SEQ, D), 1.0),
        "x_sample": nrm(ks[1], (DEC_BATCH, DEC_SEQ, D), 1.0),
        "c_prompt": nrm(ks[2], (BATCH, D), 1.0),
        "c_sample": nrm(ks[3], (DEC_BATCH, D), 1.0),
        "w_mod": nrm(ks[4], (DEPTH, D, N_MOD * D), 0.5 * D ** -0.5),
        "b_mod": nrm(ks[5], (DEPTH, N_MOD * D), 0.02),
        "norm_ffn1_g": gain(ks[6], (DEPTH, D)),
        "ffn1_w_gu": nrm(ks[7], (DEPTH, D, 2 * D_FF), D ** -0.5),
        "ffn1_w_down": nrm(ks[8], (DEPTH, D_FF, D), D_FF ** -0.5),
        "norm_mix_g": gain(ks[9], (DEPTH, D)),
        "w_in": nrm(ks[10], (DEPTH, D, IN_COLS), D ** -0.5),
        "na_q_g": gain(ks[11], (DEPTH, HEAD_DIM)),
        "na_k_g": gain(ks[12], (DEPTH, HEAD_DIM)),
        "na_rpb": nrm(ks[13], (DEPTH, NA_HEADS, RPB_ROWS, RPB_COLS), 0.1),
        "gq_q_g": gain(ks[14], (DEPTH, HEAD_DIM)),
        "gq_k_g": gain(ks[15], (DEPTH, HEAD_DIM)),
        "conv_w": nrm(ks[16], (DEPTH, CONV_WIDTH, CONV_CH), CONV_WIDTH ** -0.5),
        "conv_b": nrm(ks[17], (DEPTH, CONV_CH), 0.02),
        "conv_ln_g": gain(ks[18], (DEPTH, CONV_CH)),
        "conv_ln_b": nrm(ks[19], (DEPTH, CONV_CH), 0.02),
        "out_norm_g": gain(ks[20], (DEPTH, MIX_W)),
        "w_out": nrm(ks[21], (DEPTH, MIX_W, D), MIX_W ** -0.5),
        "norm_ffn2_g": gain(ks[22], (DEPTH, D)),
        "ffn2_w_gu": nrm(ks[23], (DEPTH, D, 2 * D_FF), D ** -0.5),
        "ffn2_w_down": nrm(ks[24], (DEPTH, D_FF, D), D_FF ** -0.5),
    }


def reference(x_prompt, x_sample, c_prompt, c_sample, w_mod, b_mod, norm_ffn1_g, ffn1_w_gu,
              ffn1_w_down, norm_mix_g, w_in, na_q_g, na_k_g, na_rpb, gq_q_g, gq_k_g, conv_w,
              conv_b, conv_ln_g, conv_ln_b, out_norm_g, w_out, norm_ffn2_g, ffn2_w_gu, ffn2_w_down):
    y_prompt = trunk(x_prompt, c_prompt, w_mod, b_mod, norm_ffn1_g, ffn1_w_gu, ffn1_w_down,
                     norm_mix_g, w_in, na_q_g, na_k_g, na_rpb, gq_q_g, gq_k_g, conv_w, conv_b,
                     conv_ln_g, conv_ln_b, out_norm_g, w_out, norm_ffn2_g, ffn2_w_gu, ffn2_w_down)
    y_sample = trunk(x_sample, c_sample, w_mod, b_mod, norm_ffn1_g, ffn1_w_gu, ffn1_w_down,
                     norm_mix_g, w_in, na_q_g, na_k_g, na_rpb, gq_q_g, gq_k_g, conv_w, conv_b,
                     conv_ln_g, conv_ln_b, out_norm_g, w_out, norm_ffn2_g, ffn2_w_gu, ffn2_w_down)
    return (y_prompt, y_sample)
```

```python
import functools

import jax
import jax.numpy as jnp
from jax import lax
from jax.experimental import pallas as pl
from jax.experimental.pallas import tpu as pltpu

F32 = jnp.float32
BF16 = jnp.bfloat16

GRID_W = 64
HEAD_DIM = 64
NA_HEADS = 4
NA_WIN_ROWS = 8
NA_WIN_COLS = 16
GQ_HEADS = 8
GQ_KV_HEADS = 2
GQ_GROUP = GQ_HEADS // GQ_KV_HEADS
CONV_CH = 256
CONV_WIDTH = 31
CONV_PAD = CONV_WIDTH // 2
ROPE_THETA = 10000.0
EPS = 1e-6
NEG_INF = -1e30
NA_W = NA_HEADS * HEAD_DIM
GQ_W = GQ_HEADS * HEAD_DIM
KV_W = GQ_KV_HEADS * HEAD_DIM
N_MOD = 9
RPB_ROWS = 2 * NA_WIN_ROWS - 1
RPB_COLS = 2 * NA_WIN_COLS - 1

LANES = 128
NA_BLOCK_ROWS = 4
NA_BLOCK = NA_BLOCK_ROWS * GRID_W
NA_SLOTS = 3 * NA_BLOCK_ROWS
CONV_HALO = 16
VMEM_LIMIT = 56 * 1024 * 1024


def _tile(n, target, align=8):
    if n <= target:
        return n
    for t in range(target - target % align, 0, -align):
        if n % t == 0:
            return t
    return n


def _params(sem):
    return pltpu.CompilerParams(dimension_semantics=sem, vmem_limit_bytes=VMEM_LIMIT)


def _sigmoid(x):
    return 1.0 / (1.0 + jnp.exp(-x))


def _mod_norm(x, g, shift, scale):
    ms = jnp.mean(x * x, axis=-1, keepdims=True)
    y = x * lax.rsqrt(ms + EPS) * g
    return y * (1.0 + scale) + shift


def _mod_kernel(c_ref, w_ref, b_ref, o_ref):
    c = c_ref[...]
    ca = (c * _sigmoid(c)).astype(BF16)
    o_ref[0] = jnp.dot(ca, w_ref[0].astype(BF16), preferred_element_type=F32) + b_ref[0]


def _mod_all(c, w_mod, b_mod):
    depth, d, nd = w_mod.shape
    bp = c.shape[0]
    tn = _tile(nd, 1024, LANES)
    return pl.pallas_call(
        _mod_kernel,
        out_shape=jax.ShapeDtypeStruct((depth, bp, nd), F32),
        grid=(depth, nd // tn),
        in_specs=[
            pl.BlockSpec((bp, d), lambda l, n: (0, 0)),
            pl.BlockSpec((1, d, tn), lambda l, n: (l, 0, n)),
            pl.BlockSpec((1, 1, tn), lambda l, n: (l, 0, n)),
        ],
        out_specs=pl.BlockSpec((1, bp, tn), lambda l, n: (l, 0, n)),
        compiler_params=_params(("arbitrary", "arbitrary")),
        name="mod_vectors",
    )(c, w_mod, b_mod.reshape(depth, 1, nd))


def _ffn_kernel(x_ref, mod_ref, g_ref, wg_ref, wu_ref, wd_ref, o_ref, h_ref, acc_ref, *, sub, nff):
    j = pl.program_id(2)

    @pl.when(j == 0)
    def _():
        h = _mod_norm(x_ref[0], g_ref[...], mod_ref[0, 3 * sub:3 * sub + 1, :],
                      mod_ref[0, 3 * sub + 1:3 * sub + 2, :])
        h_ref[...] = h.astype(BF16)
        acc_ref[...] = jnp.zeros_like(acc_ref)

    h = h_ref[...]
    gate = jnp.dot(h, wg_ref[...], preferred_element_type=F32)
    up = jnp.dot(h, wu_ref[...], preferred_element_type=F32)
    a = (gate * _sigmoid(gate) * up).astype(BF16)
    acc_ref[...] += jnp.dot(a, wd_ref[...], preferred_element_type=F32)

    @pl.when(j == nff - 1)
    def _():
        o_ref[0] = x_ref[0] + (0.5 * mod_ref[0, 3 * sub + 2:3 * sub + 3, :]) * acc_ref[...]


def _ffn(x, mod, g, w_gu, w_down, *, sub):
    b, l, d = x.shape
    dff = w_down.shape[0]
    tm = _tile(l, 512)
    nff = 2 if (dff // 2) % LANES == 0 else 1
    tf = dff // nff
    return pl.pallas_call(
        functools.partial(_ffn_kernel, sub=sub, nff=nff),
        out_shape=jax.ShapeDtypeStruct(x.shape, F32),
        grid=(b, l // tm, nff),
        in_specs=[
            pl.BlockSpec((1, tm, d), lambda bi, i, j: (bi, i, 0)),
            pl.BlockSpec((1, N_MOD, d), lambda bi, i, j: (bi, 0, 0)),
            pl.BlockSpec((1, d), lambda bi, i, j: (0, 0)),
            pl.BlockSpec((d, tf), lambda bi, i, j: (0, j)),
            pl.BlockSpec((d, tf), lambda bi, i, j: (0, j + nff)),
            pl.BlockSpec((tf, d), lambda bi, i, j: (j, 0)),
        ],
        out_specs=pl.BlockSpec((1, tm, d), lambda bi, i, j: (bi, i, 0)),
        scratch_shapes=[pltpu.VMEM((tm, d), BF16), pltpu.VMEM((tm, d), F32)],
        compiler_params=_params(("parallel", "parallel", "arbitrary")),
        name=f"ffn{sub}",
    )(x, mod, g.reshape(1, d), w_gu, w_gu, w_down)


def _head_rms(z, gain, seg):
    sq = z * z
    hi = sq.astype(BF16)
    lo = (sq - hi.astype(F32)).astype(BF16)
    ssq = jnp.dot(hi, seg, preferred_element_type=F32) + jnp.dot(lo, seg, preferred_element_type=F32)
    return z * lax.rsqrt(ssq * (1.0 / HEAD_DIM) + EPS) * gain


def _rope(x, cos, sin_signed):
    lane = lax.broadcasted_iota(jnp.int32, x.shape, 1)
    partner = jnp.where(lane % 2 == 0, pltpu.roll(x, LANES - 1, 1), pltpu.roll(x, 1, 1))
    return x * cos + partner * sin_signed


def _inproj_kernel(x_ref, mod_ref, g_ref, w_ref, gqa_ref, gka_ref, gqb_ref, gkb_ref, cos_ref, sin_ref,
                   qa_ref, ka_ref, va_ref, qb_ref, kt_ref, vx_ref, u_ref):
    h = _mod_norm(x_ref[0], g_ref[...], mod_ref[0, 3:4, :], mod_ref[0, 4:5, :]).astype(BF16)
    z = jnp.dot(h, w_ref[...], preferred_element_type=F32)
    tm = z.shape[0]
    r = lax.broadcasted_iota(jnp.int32, (2 * LANES, 2 * LANES), 0) // HEAD_DIM
    c = lax.broadcasted_iota(jnp.int32, (2 * LANES, 2 * LANES), 1) // HEAD_DIM
    seg = (r == c).astype(BF16)
    cos = cos_ref[...]
    sin = sin_ref[...]
    scale = HEAD_DIM ** -0.5

    o = 0
    qa_ref[0] = (_head_rms(z[:, o:o + NA_W], gqa_ref[...], seg) * scale).astype(BF16)
    o += NA_W
    ka_ref[0] = _head_rms(z[:, o:o + NA_W], gka_ref[...], seg).astype(BF16)
    o += NA_W
    va_ref[0] = z[:, o:o + NA_W].astype(BF16)
    o += NA_W
    for cix in range(GQ_W // (2 * LANES)):
        qn = _head_rms(z[:, o:o + 2 * LANES], gqb_ref[...], seg)
        for half in range(2):
            qr = _rope(qn[:, half * LANES:(half + 1) * LANES], cos, sin)
            lo_ = cix * 2 * LANES + half * LANES
            qb_ref[0, :, lo_:lo_ + LANES] = (qr * scale).astype(BF16)
        o += 2 * LANES
    kn = _head_rms(z[:, o:o + KV_W], gkb_ref[...], seg[:KV_W, :KV_W])
    kr = _rope(kn, cos, sin)
    kt_ref[0] = kr.T.astype(BF16).reshape(GQ_KV_HEADS, HEAD_DIM, tm)
    o += KV_W
    vb = z[:, o:o + KV_W]
    lane = lax.broadcasted_iota(jnp.int32, vb.shape, 1)
    ones_col = jnp.where(lane == HEAD_DIM, 1.0, 0.0)
    vx_ref[0, 0] = jnp.where(lane < HEAD_DIM, vb, ones_col).astype(BF16)
    vx_ref[0, 1] = jnp.where(lane < HEAD_DIM, pltpu.roll(vb, HEAD_DIM, 1), ones_col).astype(BF16)
    o += KV_W
    ca = z[:, o:o + CONV_CH]
    cb = z[:, o + CONV_CH:o + 2 * CONV_CH]
    u_ref[0] = ca * _sigmoid(cb)


def _inproj(x, mod, g, w_in, gqa, gka, gqb, gkb, cos, sin):
    b, l, d = x.shape
    ncol = w_in.shape[1]
    tm = _tile(l, 512)
    tok = lambda w: pl.BlockSpec((1, tm, w), lambda bi, i: (bi, i, 0))
    vec = lambda w: pl.BlockSpec((1, w), lambda bi, i: (0, 0))
    return pl.pallas_call(
        _inproj_kernel,
        out_shape=(
            jax.ShapeDtypeStruct((b, l, NA_W), BF16),
            jax.ShapeDtypeStruct((b, l, NA_W), BF16),
            jax.ShapeDtypeStruct((b, l, NA_W), BF16),
            jax.ShapeDtypeStruct((b, l, GQ_W), BF16),
            jax.ShapeDtypeStruct((b, GQ_KV_HEADS, HEAD_DIM, l), BF16),
            jax.ShapeDtypeStruct((b, GQ_KV_HEADS, l, LANES), BF16),
            jax.ShapeDtypeStruct((b, l, CONV_CH), F32),
        ),
        grid=(b, l // tm),
        in_specs=[
            tok(d),
            pl.BlockSpec((1, N_MOD, d), lambda bi, i: (bi, 0, 0)),
            vec(d),
            pl.BlockSpec((d, ncol), lambda bi, i: (0, 0)),
            vec(NA_W), vec(NA_W), vec(2 * LANES), vec(KV_W),
            pl.BlockSpec((tm, LANES), lambda bi, i: (i, 0)),
            pl.BlockSpec((tm, LANES), lambda bi, i: (i, 0)),
        ],
        out_specs=(
            tok(NA_W), tok(NA_W), tok(NA_W), tok(GQ_W),
            pl.BlockSpec((1, GQ_KV_HEADS, HEAD_DIM, tm), lambda bi, i: (bi, 0, 0, i)),
            pl.BlockSpec((1, GQ_KV_HEADS, tm, LANES), lambda bi, i: (bi, 0, i, 0)),
            tok(CONV_CH),
        ),
        compiler_params=_params(("parallel", "parallel")),
        name="inproj",
    )(x, mod, g.reshape(1, d), w_in, gqa, gka, gqb, gkb, cos, sin)


def _na_kernel(q_ref, kp_ref, kc_ref, kn_ref, vp_ref, vc_ref, vn_ref, bias_ref, o_ref):
    for h in range(NA_HEADS):
        sl = slice(h * HEAD_DIM, (h + 1) * HEAD_DIM)
        q = q_ref[0, :, sl]
        k = jnp.concatenate([kp_ref[0, :, sl], kc_ref[0, :, sl], kn_ref[0, :, sl]], axis=0)
        v = jnp.concatenate([vp_ref[0, :, sl], vc_ref[0, :, sl], vn_ref[0, :, sl]], axis=0)
        s = lax.dot_general(q, k, (((1,), (1,)), ((), ())), preferred_element_type=F32) + bias_ref[0, h]
        m = jnp.max(s, axis=-1, keepdims=True)
        p = jnp.exp(s - m)
        den = jnp.sum(p, axis=-1, keepdims=True)
        o = jnp.dot(p.astype(BF16), v, preferred_element_type=F32)
        o_ref[0, :, sl] = o / den


def _na_bias(rpb):
    c = jnp.arange(GRID_W)
    cs = jnp.clip(c - NA_WIN_COLS // 2, 0, GRID_W - NA_WIN_COLS)
    col_off = jnp.clip(c[None, :] - c[:, None] + (NA_WIN_COLS - 1), 0, RPB_COLS - 1)
    col_ok = (c[None, :] >= cs[:, None]) & (c[None, :] < cs[:, None] + NA_WIN_COLS)
    toep = jnp.where(col_ok[None, None], rpb.astype(F32)[:, :, col_off], NEG_INF)
    a = jnp.arange(NA_BLOCK_ROWS)[:, None]
    s = jnp.arange(NA_SLOTS)[None, :]
    rel = s - NA_BLOCK_ROWS - a
    half = NA_WIN_ROWS // 2
    valid = jnp.stack([
        jnp.broadcast_to((s >= NA_BLOCK_ROWS) & (s < NA_BLOCK_ROWS + NA_WIN_ROWS), rel.shape),
        (rel >= -half) & (rel < NA_WIN_ROWS - half),
        jnp.broadcast_to(s < NA_WIN_ROWS, rel.shape),
    ])
    ro = jnp.clip(rel + (NA_WIN_ROWS - 1), 0, RPB_ROWS - 1)
    t = toep[:, ro]
    t = jnp.where(valid[:, None, :, :, None, None], t[None], NEG_INF)
    t = t.transpose(0, 1, 2, 4, 3, 5)
    return t.reshape(3, NA_HEADS, NA_BLOCK, NA_SLOTS * GRID_W)


def _na_attention(qa, ka, va, bias):
    b, l, w = qa.shape
    n = l // NA_BLOCK
    assert l % NA_BLOCK == 0 and n >= 3, "sequence must cover at least three query blocks"
    cur = pl.BlockSpec((1, NA_BLOCK, w), lambda bi, i: (bi, i, 0))
    prev = pl.BlockSpec((1, NA_BLOCK, w), lambda bi, i: (bi, jnp.maximum(i - 1, 0), 0))
    nxt = pl.BlockSpec((1, NA_BLOCK, w), lambda bi, i: (bi, jnp.minimum(i + 1, n - 1), 0))
    case = lambda bi, i: (jnp.where(i == 0, 0, jnp.where(i == n - 1, 2, 1)), 0, 0, 0)
    return pl.pallas_call(
        _na_kernel,
        out_shape=jax.ShapeDtypeStruct((b, l, w), F32),
        grid=(b, n),
        in_specs=[cur, prev, cur, nxt, prev, cur, nxt,
                  pl.BlockSpec((1, NA_HEADS, NA_BLOCK, NA_SLOTS * GRID_W), case)],
        out_specs=cur,
        compiler_params=_params(("parallel", "parallel")),
        name="na_attention",
    )(qa, ka, ka, ka, va, va, va, bias)


def _gqa_kernel(q_ref, kt_ref, v_ref, o_ref, qs_ref, m_ref, acc_ref, *, tq, tk, nk):
    for h in range(GQ_GROUP):
        qs_ref[h * tq:(h + 1) * tq, :] = q_ref[0, :, h * HEAD_DIM:(h + 1) * HEAD_DIM]
    m_ref[...] = jnp.full(m_ref.shape, NEG_INF, F32)
    acc_ref[...] = jnp.zeros_like(acc_ref)

    def body(j, carry):
        k0 = pl.multiple_of(j * tk, tk)
        kt = kt_ref[0, 0, :, pl.ds(k0, tk)]
        v = v_ref[0, 0, pl.ds(k0, tk), :]
        s = jnp.dot(qs_ref[...], kt, preferred_element_type=F32)
        m_prev = m_ref[...]
        m_new = jnp.maximum(m_prev, jnp.max(s, axis=-1, keepdims=True))
        alpha = jnp.exp(m_prev - m_new)
        p = jnp.exp(s - jnp.tile(m_new, (1, tk // LANES)))
        acc_ref[...] = alpha * acc_ref[...] + jnp.dot(p.astype(BF16), v, preferred_element_type=F32)
        m_ref[...] = m_new
        return carry

    lax.fori_loop(0, nk, body, 0)
    for h in range(GQ_GROUP):
        a = acc_ref[h * tq:(h + 1) * tq, :]
        o_ref[0, :, h * HEAD_DIM:(h + 1) * HEAD_DIM] = a[:, :HEAD_DIM] / a[:, HEAD_DIM:HEAD_DIM + 1]


def _gqa_attention(qb, kt, vx):
    b, l, w = qb.shape
    tq = _tile(l, 256)
    tk = _tile(l, 512, LANES)
    gw = GQ_GROUP * HEAD_DIM
    rows = GQ_GROUP * tq
    return pl.pallas_call(
        functools.partial(_gqa_kernel, tq=tq, tk=tk, nk=l // tk),
        out_shape=jax.ShapeDtypeStruct((b, l, w), F32),
        grid=(b, GQ_KV_HEADS, l // tq),
        in_specs=[
            pl.BlockSpec((1, tq, gw), lambda bi, kv, i: (bi, i, kv)),
            pl.BlockSpec((1, 1, HEAD_DIM, l), lambda bi, kv, i: (bi, kv, 0, 0)),
            pl.BlockSpec((1, 1, l, LANES), lambda bi, kv, i: (bi, kv, 0, 0)),
        ],
        out_specs=pl.BlockSpec((1, tq, gw), lambda bi, kv, i: (bi, i, kv)),
        scratch_shapes=[pltpu.VMEM((rows, HEAD_DIM), BF16), pltpu.VMEM((rows, LANES), F32),
                        pltpu.VMEM((rows, LANES), F32)],
        compiler_params=_params(("parallel", "parallel", "arbitrary")),
        name="gqa_attention",
    )(qb, kt, vx)


def _conv_kernel(u_ref, up_ref, un_ref, w_ref, b_ref, g_ref, beta_ref, o_ref, ext_ref, *, t, nt, rc):
    i = pl.program_id(1)
    ext_ref[0:CONV_HALO, :] = jnp.where(i > 0, up_ref[0], 0.0)
    ext_ref[CONV_HALO:CONV_HALO + t, :] = u_ref[0]
    ext_ref[CONV_HALO + t:, :] = jnp.where(i < nt - 1, un_ref[0], 0.0)
    base = CONV_HALO - CONV_PAD
    for c0 in range(0, t, rc):
        acc = jnp.zeros((rc, CONV_CH), F32)
        for k in range(CONV_WIDTH):
            acc = acc + ext_ref[base + c0 + k:base + c0 + k + rc, :] * w_ref[k:k + 1, :]
        y = acc + b_ref[...]
        mu = jnp.mean(y, axis=-1, keepdims=True)
        yc = y - mu
        var = jnp.mean(yc * yc, axis=-1, keepdims=True)
        yn = yc * lax.rsqrt(var + EPS) * g_ref[...] + beta_ref[...]
        o_ref[0, c0:c0 + rc, :] = yn * _sigmoid(yn)


def _conv(u, w, bias, ln_g, ln_b):
    b, l, ch = u.shape
    t = _tile(l, 512)
    nt = l // t
    rc = _tile(t, 64)
    hb = t // CONV_HALO
    nh = l // CONV_HALO
    vec = pl.BlockSpec((1, ch), lambda bi, i: (0, 0))
    return pl.pallas_call(
        functools.partial(_conv_kernel, t=t, nt=nt, rc=rc),
        out_shape=jax.ShapeDtypeStruct(u.shape, F32),
        grid=(b, nt),
        in_specs=[
            pl.BlockSpec((1, t, ch), lambda bi, i: (bi, i, 0)),
            pl.BlockSpec((1, CONV_HALO, ch), lambda bi, i: (bi, jnp.maximum(i * hb - 1, 0), 0)),
            pl.BlockSpec((1, CONV_HALO, ch), lambda bi, i: (bi, jnp.minimum((i + 1) * hb, nh - 1), 0)),
            pl.BlockSpec((CONV_WIDTH, ch), lambda bi, i: (0, 0)),
            vec, vec, vec,
        ],
        out_specs=pl.BlockSpec((1, t, ch), lambda bi, i: (bi, i, 0)),
        scratch_shapes=[pltpu.VMEM((t + 2 * CONV_HALO, ch), F32)],
        compiler_params=_params(("parallel", "parallel")),
        name="conformer_conv",
    )(u, u, u, w, bias.reshape(1, ch), ln_g.reshape(1, ch), ln_b.reshape(1, ch))


def _group_rms(y, g):
    ms = jnp.mean(y * y, axis=-1, keepdims=True)
    return (y * lax.rsqrt(ms + EPS) * g).astype(BF16)


def _outproj_kernel(x_ref, ya_ref, yb_ref, yc_ref, mod_ref, g_ref, w_ref, o_ref):
    y = jnp.dot(_group_rms(ya_ref[0], g_ref[:, :NA_W]), w_ref[:NA_W, :], preferred_element_type=F32)
    y += jnp.dot(_group_rms(yb_ref[0], g_ref[:, NA_W:NA_W + GQ_W]), w_ref[NA_W:NA_W + GQ_W, :],
                 preferred_element_type=F32)
    y += jnp.dot(_group_rms(yc_ref[0], g_ref[:, NA_W + GQ_W:]), w_ref[NA_W + GQ_W:, :],
                 preferred_element_type=F32)
    o_ref[0] = x_ref[0] + mod_ref[0, 5:6, :] * y


def _outproj(x, ya, yb, yc, mod, og, w_out):
    b, l, d = x.shape
    mix = w_out.shape[0]
    tm = _tile(l, 512)
    tok = lambda w: pl.BlockSpec((1, tm, w), lambda bi, i: (bi, i, 0))
    return pl.pallas_call(
        _outproj_kernel,
        out_shape=jax.ShapeDtypeStruct(x.shape, F32),
        grid=(b, l // tm),
        in_specs=[
            tok(d), tok(NA_W), tok(GQ_W), tok(CONV_CH),
            pl.BlockSpec((1, N_MOD, d), lambda bi, i: (bi, 0, 0)),
            pl.BlockSpec((1, mix), lambda bi, i: (0, 0)),
            pl.BlockSpec((mix, d), lambda bi, i: (0, 0)),
        ],
        out_specs=tok(d),
        compiler_params=_params(("parallel", "parallel")),
        name="outproj",
    )(x, ya, yb, yc, mod, og.reshape(1, mix), w_out)


def _rope_tables(l):
    t = jnp.arange(l)
    row = (t // GRID_W).astype(F32)
    col = (t % GRID_W).astype(F32)
    half = HEAD_DIM // 2
    freqs = ROPE_THETA ** (-jnp.arange(0, half, 2, dtype=F32) / half)
    ang = jnp.concatenate([row[:, None] * freqs, col[:, None] * freqs], axis=-1)
    cos = jnp.repeat(jnp.cos(ang), 2, axis=-1)
    sin = jnp.repeat(jnp.sin(ang), 2, axis=-1)
    sign = jnp.where(jnp.arange(HEAD_DIM) % 2 == 0, -1.0, 1.0).astype(F32)
    reps = LANES // HEAD_DIM
    return jnp.tile(cos, (1, reps)), jnp.tile(sin * sign, (1, reps))


def _trunk(x, mods, layers):
    l = x.shape[1]
    cos, sin = _rope_tables(l)
    for mod, p in zip(mods, layers):
        x = _ffn(x, mod, p["norm_ffn1_g"], p["ffn1_w_gu"], p["ffn1_w_down"], sub=0)
        qa, ka, va, qb, kt, vx, u = _inproj(x, mod, p["norm_mix_g"], p["w_in"], p["gqa"], p["gka"],
                                            p["gqb"], p["gkb"], cos, sin)
        ya = _na_attention(qa, ka, va, p["na_bias"])
        yb = _gqa_attention(qb, kt, vx)
        yc = _conv(u, p["conv_w"], p["conv_b"], p["conv_ln_g"], p["conv_ln_b"])
        x = _outproj(x, ya, yb, yc, mod, p["out_norm_g"], p["w_out"])
        x = _ffn(x, mod, p["norm_ffn2_g"], p["ffn2_w_gu"], p["ffn2_w_down"], sub=2)
    return x


def kernel(x_prompt, x_sample, c_prompt, c_sample, w_mod, b_mod, norm_ffn1_g, ffn1_w_gu, ffn1_w_down,
           norm_mix_g, w_in, na_q_g, na_k_g, na_rpb, gq_q_g, gq_k_g, conv_w, conv_b, conv_ln_g, conv_ln_b,
           out_norm_g, w_out, norm_ffn2_g, ffn2_w_gu, ffn2_w_down):
    depth, d = norm_ffn1_g.shape
    bp = c_prompt.shape[0]
    bs = c_sample.shape[0]
    pad = (-(bp + bs)) % 8
    c_all = jnp.concatenate([c_prompt, c_sample, jnp.zeros((pad, d), F32)], axis=0)
    mod_all = _mod_all(c_all, w_mod, b_mod).reshape(depth, bp + bs + pad, N_MOD, d)

    tile_gain = lambda g, w: jnp.tile(g.astype(F32), w // HEAD_DIM).reshape(1, w)
    layers = []
    for i in range(depth):
        layers.append(dict(
            norm_ffn1_g=norm_ffn1_g[i], ffn1_w_gu=ffn1_w_gu[i].astype(BF16),
            ffn1_w_down=ffn1_w_down[i].astype(BF16), norm_mix_g=norm_mix_g[i], w_in=w_in[i].astype(BF16),
            gqa=tile_gain(na_q_g[i], NA_W), gka=tile_gain(na_k_g[i], NA_W),
            gqb=tile_gain(gq_q_g[i], 2 * LANES), gkb=tile_gain(gq_k_g[i], KV_W),
            na_bias=_na_bias(na_rpb[i]), conv_w=conv_w[i], conv_b=conv_b[i], conv_ln_g=conv_ln_g[i],
            conv_ln_b=conv_ln_b[i], out_norm_g=out_norm_g[i], w_out=w_out[i].astype(BF16),
            norm_ffn2_g=norm_ffn2_g[i], ffn2_w_gu=ffn2_w_gu[i].astype(BF16),
            ffn2_w_down=ffn2_w_down[i].astype(BF16)))
    y_prompt = _trunk(x_prompt, [mod_all[i, :bp] for i in range(depth)], layers)
    y_sample = _trunk(x_sample, [mod_all[i, bp:bp + bs] for i in range(depth)], layers)
    return (y_prompt, y_sample)
```

```python
import functools

import jax
import jax.numpy as jnp
from jax import lax
from jax.experimental import pallas as pl
from jax.experimental.pallas import tpu as pltpu

F32 = jnp.float32
BF16 = jnp.bfloat16

GRID_W = 64
HEAD_DIM = 64
NA_HEADS = 4
NA_WIN_ROWS = 8
NA_WIN_COLS = 16
GQ_HEADS = 8
GQ_KV_HEADS = 2
GQ_GROUP = GQ_HEADS // GQ_KV_HEADS
CONV_CH = 256
CONV_WIDTH = 31
CONV_PAD = CONV_WIDTH // 2
ROPE_THETA = 10000.0
EPS = 1e-6
NEG_INF = -1e30
NA_W = NA_HEADS * HEAD_DIM
GQ_W = GQ_HEADS * HEAD_DIM
KV_W = GQ_KV_HEADS * HEAD_DIM
N_MOD = 9
RPB_ROWS = 2 * NA_WIN_ROWS - 1
RPB_COLS = 2 * NA_WIN_COLS - 1

LANES = 128
NA_BLOCK_ROWS = 4
NA_BLOCK = NA_BLOCK_ROWS * GRID_W
NA_SLOTS = 3 * NA_BLOCK_ROWS
GQA_KEY_CHUNK = 1024
CONV_HALO = 16
VMEM_LIMIT = 56 * 1024 * 1024


def _tile(n, target, align=8):
    if n <= target:
        return n
    for t in range(target - target % align, 0, -align):
        if n % t == 0:
            return t
    return n


def _params(sem):
    return pltpu.CompilerParams(dimension_semantics=sem, vmem_limit_bytes=VMEM_LIMIT)


def _sigmoid(x):
    return 1.0 / (1.0 + jnp.exp(-x))


def _mod_norm(x, g, shift, scale):
    ms = jnp.mean(x * x, axis=-1, keepdims=True)
    y = x * lax.rsqrt(ms + EPS) * g
    return y * (1.0 + scale) + shift


def _mod_kernel(c_ref, w_ref, b_ref, o_ref):
    c = c_ref[...]
    ca = (c * _sigmoid(c)).astype(BF16)
    o_ref[0] = jnp.dot(ca, w_ref[0].astype(BF16), preferred_element_type=F32) + b_ref[0]


def _mod_all(c, w_mod, b_mod):
    depth, d, nd = w_mod.shape
    bp = c.shape[0]
    tn = _tile(nd, 1024, LANES)
    return pl.pallas_call(
        _mod_kernel,
        out_shape=jax.ShapeDtypeStruct((depth, bp, nd), F32),
        grid=(depth, nd // tn),
        in_specs=[
            pl.BlockSpec((bp, d), lambda l, n: (0, 0)),
            pl.BlockSpec((1, d, tn), lambda l, n: (l, 0, n)),
            pl.BlockSpec((1, 1, tn), lambda l, n: (l, 0, n)),
        ],
        out_specs=pl.BlockSpec((1, bp, tn), lambda l, n: (l, 0, n)),
        compiler_params=_params(("arbitrary", "arbitrary")),
        name="mod_vectors",
    )(c, w_mod, b_mod.reshape(depth, 1, nd))


def _ffn_kernel(x_ref, mod_ref, g_ref, wgu_ref, wd_ref, o_ref, *, sub, dff):
    x = x_ref[0]
    h = _mod_norm(x, g_ref[...], mod_ref[0, 3 * sub:3 * sub + 1, :],
                  mod_ref[0, 3 * sub + 1:3 * sub + 2, :]).astype(BF16)
    gate = jnp.dot(h, wgu_ref[:, :dff], preferred_element_type=F32)
    up = jnp.dot(h, wgu_ref[:, dff:], preferred_element_type=F32)
    a = (gate * _sigmoid(gate) * up).astype(BF16)
    y = jnp.dot(a, wd_ref[...], preferred_element_type=F32)
    o_ref[0] = x + (0.5 * mod_ref[0, 3 * sub + 2:3 * sub + 3, :]) * y


def _ffn(x, mod, g, w_gu, w_down, *, sub):
    b, l, d = x.shape
    dff = w_down.shape[0]
    tm = _tile(l, 512)
    resident = pl.Buffered(1)
    return pl.pallas_call(
        functools.partial(_ffn_kernel, sub=sub, dff=dff),
        out_shape=jax.ShapeDtypeStruct(x.shape, F32),
        grid=(b, l // tm),
        in_specs=[
            pl.BlockSpec((1, tm, d), lambda bi, i: (bi, i, 0)),
            pl.BlockSpec((1, N_MOD, d), lambda bi, i: (bi, 0, 0)),
            pl.BlockSpec((1, d), lambda bi, i: (0, 0)),
            pl.BlockSpec((d, 2 * dff), lambda bi, i: (0, 0), pipeline_mode=resident),
            pl.BlockSpec((dff, d), lambda bi, i: (0, 0), pipeline_mode=resident),
        ],
        out_specs=pl.BlockSpec((1, tm, d), lambda bi, i: (bi, i, 0)),
        compiler_params=_params(("parallel", "parallel")),
        name=f"ffn{sub}",
    )(x, mod, g.reshape(1, d), w_gu, w_down)


def _head_rms(z, gain, seg):
    sq = z * z
    hi = sq.astype(BF16)
    lo = (sq - hi.astype(F32)).astype(BF16)
    ssq = jnp.dot(hi, seg, preferred_element_type=F32) + jnp.dot(lo, seg, preferred_element_type=F32)
    return z * lax.rsqrt(ssq * (1.0 / HEAD_DIM) + EPS) * gain


def _rope(x, cos, sin_signed):
    lane = lax.broadcasted_iota(jnp.int32, x.shape, 1)
    partner = jnp.where(lane % 2 == 0, pltpu.roll(x, LANES - 1, 1), pltpu.roll(x, 1, 1))
    return x * cos + partner * sin_signed


def _inproj_kernel(x_ref, mod_ref, g_ref, w_ref, gqa_ref, gka_ref, gqb_ref, gkb_ref, cos_ref, sin_ref,
                   qa_ref, ka_ref, va_ref, qb_ref, kt_ref, vx_ref, u_ref):
    h = _mod_norm(x_ref[0], g_ref[...], mod_ref[0, 3:4, :], mod_ref[0, 4:5, :]).astype(BF16)
    z = jnp.dot(h, w_ref[...], preferred_element_type=F32)
    tm = z.shape[0]
    r = lax.broadcasted_iota(jnp.int32, (2 * LANES, 2 * LANES), 0) // HEAD_DIM
    c = lax.broadcasted_iota(jnp.int32, (2 * LANES, 2 * LANES), 1) // HEAD_DIM
    seg = (r == c).astype(BF16)
    cos = cos_ref[...]
    sin = sin_ref[...]
    scale = HEAD_DIM ** -0.5

    o = 0
    qa_ref[0] = (_head_rms(z[:, o:o + NA_W], gqa_ref[...], seg) * scale).astype(BF16)
    o += NA_W
    ka_ref[0] = _head_rms(z[:, o:o + NA_W], gka_ref[...], seg).astype(BF16)
    o += NA_W
    va_ref[0] = z[:, o:o + NA_W].astype(BF16)
    o += NA_W
    for cix in range(GQ_W // (2 * LANES)):
        qn = _head_rms(z[:, o:o + 2 * LANES], gqb_ref[...], seg)
        for half in range(2):
            qr = _rope(qn[:, half * LANES:(half + 1) * LANES], cos, sin)
            lo_ = cix * 2 * LANES + half * LANES
            qb_ref[0, :, lo_:lo_ + LANES] = (qr * scale).astype(BF16)
        o += 2 * LANES
    kn = _head_rms(z[:, o:o + KV_W], gkb_ref[...], seg[:KV_W, :KV_W])
    kr = _rope(kn, cos, sin)
    kt_ref[0] = kr.T.astype(BF16).reshape(GQ_KV_HEADS, HEAD_DIM, tm)
    o += KV_W
    vb = z[:, o:o + KV_W]
    lane = lax.broadcasted_iota(jnp.int32, vb.shape, 1)
    ones_col = jnp.where(lane == HEAD_DIM, 1.0, 0.0)
    vx_ref[0, 0] = jnp.where(lane < HEAD_DIM, vb, ones_col).astype(BF16)
    vx_ref[0, 1] = jnp.where(lane < HEAD_DIM, pltpu.roll(vb, HEAD_DIM, 1), ones_col).astype(BF16)
    o += KV_W
    ca = z[:, o:o + CONV_CH]
    cb = z[:, o + CONV_CH:o + 2 * CONV_CH]
    u_ref[0] = ca * _sigmoid(cb)


def _inproj(x, mod, g, w_in, gqa, gka, gqb, gkb, cos, sin):
    b, l, d = x.shape
    ncol = w_in.shape[1]
    tm = _tile(l, 512)
    tok = lambda w: pl.BlockSpec((1, tm, w), lambda bi, i: (bi, i, 0))
    vec = lambda w: pl.BlockSpec((1, w), lambda bi, i: (0, 0))
    return pl.pallas_call(
        _inproj_kernel,
        out_shape=(
            jax.ShapeDtypeStruct((b, l, NA_W), BF16),
            jax.ShapeDtypeStruct((b, l, NA_W), BF16),
            jax.ShapeDtypeStruct((b, l, NA_W), BF16),
            jax.ShapeDtypeStruct((b, l, GQ_W), BF16),
            jax.ShapeDtypeStruct((b, GQ_KV_HEADS, HEAD_DIM, l), BF16),
            jax.ShapeDtypeStruct((b, GQ_KV_HEADS, l, LANES), BF16),
            jax.ShapeDtypeStruct((b, l, CONV_CH), F32),
        ),
        grid=(b, l // tm),
        in_specs=[
            tok(d),
            pl.BlockSpec((1, N_MOD, d), lambda bi, i: (bi, 0, 0)),
            vec(d),
            pl.BlockSpec((d, ncol), lambda bi, i: (0, 0)),
            vec(NA_W), vec(NA_W), vec(2 * LANES), vec(KV_W),
            pl.BlockSpec((tm, LANES), lambda bi, i: (i, 0)),
            pl.BlockSpec((tm, LANES), lambda bi, i: (i, 0)),
        ],
        out_specs=(
            tok(NA_W), tok(NA_W), tok(NA_W), tok(GQ_W),
            pl.BlockSpec((1, GQ_KV_HEADS, HEAD_DIM, tm), lambda bi, i: (bi, 0, 0, i)),
            pl.BlockSpec((1, GQ_KV_HEADS, tm, LANES), lambda bi, i: (bi, 0, i, 0)),
            tok(CONV_CH),
        ),
        compiler_params=_params(("parallel", "parallel")),
        name="inproj",
    )(x, mod, g.reshape(1, d), w_in, gqa, gka, gqb, gkb, cos, sin)


def _na_kernel(q_ref, kp_ref, kc_ref, kn_ref, vp_ref, vc_ref, vn_ref, bias_ref, o_ref):
    for h in range(NA_HEADS):
        sl = slice(h * HEAD_DIM, (h + 1) * HEAD_DIM)
        q = q_ref[0, :, sl]
        k = jnp.concatenate([kp_ref[0, :, sl], kc_ref[0, :, sl], kn_ref[0, :, sl]], axis=0)
        v = jnp.concatenate([vp_ref[0, :, sl], vc_ref[0, :, sl], vn_ref[0, :, sl]], axis=0)
        s = lax.dot_general(q, k, (((1,), (1,)), ((), ())), preferred_element_type=F32) + bias_ref[0, h]
        m = jnp.max(s, axis=-1, keepdims=True)
        p = jnp.exp(s - m)
        den = jnp.sum(p, axis=-1, keepdims=True)
        o = jnp.dot(p.astype(BF16), v, preferred_element_type=F32)
        o_ref[0, :, sl] = o / den


def _na_bias(rpb):
    c = jnp.arange(GRID_W)
    cs = jnp.clip(c - NA_WIN_COLS // 2, 0, GRID_W - NA_WIN_COLS)
    col_off = jnp.clip(c[None, :] - c[:, None] + (NA_WIN_COLS - 1), 0, RPB_COLS - 1)
    col_ok = (c[None, :] >= cs[:, None]) & (c[None, :] < cs[:, None] + NA_WIN_COLS)
    toep = jnp.where(col_ok[None, None], rpb.astype(F32)[:, :, col_off], NEG_INF)
    a = jnp.arange(NA_BLOCK_ROWS)[:, None]
    s = jnp.arange(NA_SLOTS)[None, :]
    rel = s - NA_BLOCK_ROWS - a
    half = NA_WIN_ROWS // 2
    valid = jnp.stack([
        jnp.broadcast_to((s >= NA_BLOCK_ROWS) & (s < NA_BLOCK_ROWS + NA_WIN_ROWS), rel.shape),
        (rel >= -half) & (rel < NA_WIN_ROWS - half),
        jnp.broadcast_to(s < NA_WIN_ROWS, rel.shape),
    ])
    ro = jnp.clip(rel + (NA_WIN_ROWS - 1), 0, RPB_ROWS - 1)
    t = toep[:, ro]
    t = jnp.where(valid[:, None, :, :, None, None], t[None], NEG_INF)
    t = t.transpose(0, 1, 2, 4, 3, 5)
    return t.reshape(3, NA_HEADS, NA_BLOCK, NA_SLOTS * GRID_W)


def _na_attention(qa, ka, va, bias):
    b, l, w = qa.shape
    n = l // NA_BLOCK
    assert l % NA_BLOCK == 0 and n >= 3, "sequence must cover at least three query blocks"
    cur = pl.BlockSpec((1, NA_BLOCK, w), lambda bi, i: (bi, i, 0))
    prev = pl.BlockSpec((1, NA_BLOCK, w), lambda bi, i: (bi, jnp.maximum(i - 1, 0), 0))
    nxt = pl.BlockSpec((1, NA_BLOCK, w), lambda bi, i: (bi, jnp.minimum(i + 1, n - 1), 0))
    case = lambda bi, i: (jnp.where(i == 0, 0, jnp.where(i == n - 1, 2, 1)), 0, 0, 0)
    return pl.pallas_call(
        _na_kernel,
        out_shape=jax.ShapeDtypeStruct((b, l, w), F32),
        grid=(b, n),
        in_specs=[cur, prev, cur, nxt, prev, cur, nxt,
                  pl.BlockSpec((1, NA_HEADS, NA_BLOCK, NA_SLOTS * GRID_W), case)],
        out_specs=cur,
        compiler_params=_params(("parallel", "parallel")),
        name="na_attention",
    )(qa, ka, ka, ka, va, va, va, bias)


def _gqa_kernel(q_ref, kt_ref, v_ref, o_ref, qs_ref, s_ref, m_ref, acc_ref, *, tq, tk, nk):
    for h in range(GQ_GROUP):
        qs_ref[h * tq:(h + 1) * tq, :] = q_ref[0, :, h * HEAD_DIM:(h + 1) * HEAD_DIM]
    m_ref[...] = jnp.full(m_ref.shape, NEG_INF, F32)
    acc_ref[...] = jnp.zeros_like(acc_ref)

    def scores(j, slot):
        k0 = pl.multiple_of(j * tk, tk)
        s_ref[slot] = jnp.dot(qs_ref[...], kt_ref[0, 0, :, pl.ds(k0, tk)], preferred_element_type=F32)

    def accumulate(j, slot):
        k0 = pl.multiple_of(j * tk, tk)
        s = s_ref[slot]
        m_prev = m_ref[...]
        m_new = jnp.maximum(m_prev, jnp.max(s, axis=-1, keepdims=True))
        alpha = jnp.exp(m_prev - m_new)
        p = jnp.exp(s - jnp.tile(m_new, (1, tk // LANES)))
        pv = jnp.dot(p.astype(BF16), v_ref[0, 0, pl.ds(k0, tk), :], preferred_element_type=F32)
        acc_ref[...] = alpha * acc_ref[...] + pv
        m_ref[...] = m_new

    scores(0, 0)

    def pair(jj, carry):
        j = 2 * jj
        scores(j + 1, 1)
        accumulate(j, 0)
        scores(j + 2, 0)
        accumulate(j + 1, 1)
        return carry

    lax.fori_loop(0, nk // 2 - 1, pair, 0)
    scores(nk - 1, 1)
    accumulate(nk - 2, 0)
    accumulate(nk - 1, 1)
    for h in range(GQ_GROUP):
        a = acc_ref[h * tq:(h + 1) * tq, :]
        o_ref[0, :, h * HEAD_DIM:(h + 1) * HEAD_DIM] = a[:, :HEAD_DIM] / a[:, HEAD_DIM:HEAD_DIM + 1]


def _gqa_attention(qb, kt, vx):
    b, l, w = qb.shape
    tq = _tile(l, 256)
    tk = _tile(l // 2, GQA_KEY_CHUNK, LANES)
    nk = l // tk
    assert nk % 2 == 0, "key chunks are consumed in pairs"
    gw = GQ_GROUP * HEAD_DIM
    rows = GQ_GROUP * tq
    return pl.pallas_call(
        functools.partial(_gqa_kernel, tq=tq, tk=tk, nk=nk),
        out_shape=jax.ShapeDtypeStruct((b, l, w), F32),
        grid=(b, GQ_KV_HEADS, l // tq),
        in_specs=[
            pl.BlockSpec((1, tq, gw), lambda bi, kv, i: (bi, i, kv)),
            pl.BlockSpec((1, 1, HEAD_DIM, l), lambda bi, kv, i: (bi, kv, 0, 0)),
            pl.BlockSpec((1, 1, l, LANES), lambda bi, kv, i: (bi, kv, 0, 0)),
        ],
        out_specs=pl.BlockSpec((1, tq, gw), lambda bi, kv, i: (bi, i, kv)),
        scratch_shapes=[pltpu.VMEM((rows, HEAD_DIM), BF16), pltpu.VMEM((2, rows, tk), F32),
                        pltpu.VMEM((rows, LANES), F32), pltpu.VMEM((rows, LANES), F32)],
        compiler_params=_params(("parallel", "parallel", "arbitrary")),
        name="gqa_attention",
    )(qb, kt, vx)


def _conv_kernel(u_ref, up_ref, un_ref, w_ref, b_ref, g_ref, beta_ref, o_ref, ext_ref, *, t, nt, rc):
    i = pl.program_id(1)
    ext_ref[0:CONV_HALO, :] = jnp.where(i > 0, up_ref[0], 0.0)
    ext_ref[CONV_HALO:CONV_HALO + t, :] = u_ref[0]
    ext_ref[CONV_HALO + t:, :] = jnp.where(i < nt - 1, un_ref[0], 0.0)
    base = CONV_HALO - CONV_PAD
    for c0 in range(0, t, rc):
        acc = jnp.zeros((rc, CONV_CH), F32)
        for k in range(CONV_WIDTH):
            acc = acc + ext_ref[base + c0 + k:base + c0 + k + rc, :] * w_ref[k:k + 1, :]
        y = acc + b_ref[...]
        mu = jnp.mean(y, axis=-1, keepdims=True)
        yc = y - mu
        var = jnp.mean(yc * yc, axis=-1, keepdims=True)
        yn = yc * lax.rsqrt(var + EPS) * g_ref[...] + beta_ref[...]
        o_ref[0, c0:c0 + rc, :] = yn * _sigmoid(yn)


def _conv(u, w, bias, ln_g, ln_b):
    b, l, ch = u.shape
    t = _tile(l, 512)
    nt = l // t
    rc = _tile(t, 64)
    hb = t // CONV_HALO
    nh = l // CONV_HALO
    vec = pl.BlockSpec((1, ch), lambda bi, i: (0, 0))
    return pl.pallas_call(
        functools.partial(_conv_kernel, t=t, nt=nt, rc=rc),
        out_shape=jax.ShapeDtypeStruct(u.shape, F32),
        grid=(b, nt),
        in_specs=[
            pl.BlockSpec((1, t, ch), lambda bi, i: (bi, i, 0)),
            pl.BlockSpec((1, CONV_HALO, ch), lambda bi, i: (bi, jnp.maximum(i * hb - 1, 0), 0)),
            pl.BlockSpec((1, CONV_HALO, ch), lambda bi, i: (bi, jnp.minimum((i + 1) * hb, nh - 1), 0)),
            pl.BlockSpec((CONV_WIDTH, ch), lambda bi, i: (0, 0)),
            vec, vec, vec,
        ],
        out_specs=pl.BlockSpec((1, t, ch), lambda bi, i: (bi, i, 0)),
        scratch_shapes=[pltpu.VMEM((t + 2 * CONV_HALO, ch), F32)],
        compiler_params=_params(("parallel", "parallel")),
        name="conformer_conv",
    )(u, u, u, w, bias.reshape(1, ch), ln_g.reshape(1, ch), ln_b.reshape(1, ch))


def _group_rms(y, g):
    ms = jnp.mean(y * y, axis=-1, keepdims=True)
    return (y * lax.rsqrt(ms + EPS) * g).astype(BF16)


def _outproj_kernel(x_ref, ya_ref, yb_ref, yc_ref, mod_ref, g_ref, w_ref, o_ref):
    y = jnp.dot(_group_rms(ya_ref[0], g_ref[:, :NA_W]), w_ref[:NA_W, :], preferred_element_type=F32)
    y += jnp.dot(_group_rms(yb_ref[0], g_ref[:, NA_W:NA_W + GQ_W]), w_ref[NA_W:NA_W + GQ_W, :],
                 preferred_element_type=F32)
    y += jnp.dot(_group_rms(yc_ref[0], g_ref[:, NA_W + GQ_W:]), w_ref[NA_W + GQ_W:, :],
                 preferred_element_type=F32)
    o_ref[0] = x_ref[0] + mod_ref[0, 5:6, :] * y


def _outproj(x, ya, yb, yc, mod, og, w_out):
    b, l, d = x.shape
    mix = w_out.shape[0]
    tm = _tile(l, 512)
    tok = lambda w: pl.BlockSpec((1, tm, w), lambda bi, i: (bi, i, 0))
    return pl.pallas_call(
        _outproj_kernel,
        out_shape=jax.ShapeDtypeStruct(x.shape, F32),
        grid=(b, l // tm),
        in_specs=[
            tok(d), tok(NA_W), tok(GQ_W), tok(CONV_CH),
            pl.BlockSpec((1, N_MOD, d), lambda bi, i: (bi, 0, 0)),
            pl.BlockSpec((1, mix), lambda bi, i: (0, 0)),
            pl.BlockSpec((mix, d), lambda bi, i: (0, 0)),
        ],
        out_specs=tok(d),
        compiler_params=_params(("parallel", "parallel")),
        name="outproj",
    )(x, ya, yb, yc, mod, og.reshape(1, mix), w_out)


def _rope_tables(l):
    t = jnp.arange(l)
    row = (t // GRID_W).astype(F32)
    col = (t % GRID_W).astype(F32)
    half = HEAD_DIM // 2
    freqs = ROPE_THETA ** (-jnp.arange(0, half, 2, dtype=F32) / half)
    ang = jnp.concatenate([row[:, None] * freqs, col[:, None] * freqs], axis=-1)
    cos = jnp.repeat(jnp.cos(ang), 2, axis=-1)
    sin = jnp.repeat(jnp.sin(ang), 2, axis=-1)
    sign = jnp.where(jnp.arange(HEAD_DIM) % 2 == 0, -1.0, 1.0).astype(F32)
    reps = LANES // HEAD_DIM
    return jnp.tile(cos, (1, reps)), jnp.tile(sin * sign, (1, reps))


def _trunk(x, mods, layers):
    l = x.shape[1]
    cos, sin = _rope_tables(l)
    for mod, p in zip(mods, layers):
        x = _ffn(x, mod, p["norm_ffn1_g"], p["ffn1_w_gu"], p["ffn1_w_down"], sub=0)
        qa, ka, va, qb, kt, vx, u = _inproj(x, mod, p["norm_mix_g"], p["w_in"], p["gqa"], p["gka"],
                                            p["gqb"], p["gkb"], cos, sin)
        ya = _na_attention(qa, ka, va, p["na_bias"])
        yb = _gqa_attention(qb, kt, vx)
        yc = _conv(u, p["conv_w"], p["conv_b"], p["conv_ln_g"], p["conv_ln_b"])
        x = _outproj(x, ya, yb, yc, mod, p["out_norm_g"], p["w_out"])
        x = _ffn(x, mod, p["norm_ffn2_g"], p["ffn2_w_gu"], p["ffn2_w_down"], sub=2)
    return x


def kernel(x_prompt, x_sample, c_prompt, c_sample, w_mod, b_mod, norm_ffn1_g, ffn1_w_gu, ffn1_w_down,
           norm_mix_g, w_in, na_q_g, na_k_g, na_rpb, gq_q_g, gq_k_g, conv_w, conv_b, conv_ln_g, conv_ln_b,
           out_norm_g, w_out, norm_ffn2_g, ffn2_w_gu, ffn2_w_down):
    depth, d = norm_ffn1_g.shape
    bp = c_prompt.shape[0]
    bs = c_sample.shape[0]
    pad = (-(bp + bs)) % 8
    c_all = jnp.concatenate([c_prompt, c_sample, jnp.zeros((pad, d), F32)], axis=0)
    mod_all = _mod_all(c_all, w_mod, b_mod).reshape(depth, bp + bs + pad, N_MOD, d)

    tile_gain = lambda g, w: jnp.tile(g.astype(F32), w // HEAD_DIM).reshape(1, w)
    layers = []
    for i in range(depth):
        layers.append(dict(
            norm_ffn1_g=norm_ffn1_g[i], ffn1_w_gu=ffn1_w_gu[i].astype(BF16),
            ffn1_w_down=ffn1_w_down[i].astype(BF16), norm_mix_g=norm_mix_g[i], w_in=w_in[i].astype(BF16),
            gqa=tile_gain(na_q_g[i], NA_W), gka=tile_gain(na_k_g[i], NA_W),
            gqb=tile_gain(gq_q_g[i], 2 * LANES), gkb=tile_gain(gq_k_g[i], KV_W),
            na_bias=_na_bias(na_rpb[i]), conv_w=conv_w[i], conv_b=conv_b[i], conv_ln_g=conv_ln_g[i],
            conv_ln_b=conv_ln_b[i], out_norm_g=out_norm_g[i], w_out=w_out[i].astype(BF16),
            norm_ffn2_g=norm_ffn2_g[i], ffn2_w_gu=ffn2_w_gu[i].astype(BF16),
            ffn2_w_down=ffn2_w_down[i].astype(BF16)))
    y_prompt = _trunk(x_prompt, [mod_all[i, :bp] for i in range(depth)], layers)
    y_sample = _trunk(x_sample, [mod_all[i, bp:bp + bs] for i in range(depth)], layers)
    return (y_prompt, y_sample)
```

```python
import functools

import jax
import jax.numpy as jnp
from jax import lax
from jax.experimental import pallas as pl
from jax.experimental.pallas import tpu as pltpu

F32 = jnp.float32
BF16 = jnp.bfloat16

GRID_W = 64
HEAD_DIM = 64
NA_HEADS = 4
NA_WIN_ROWS = 8
NA_WIN_COLS = 16
GQ_HEADS = 8
GQ_KV_HEADS = 2
GQ_GROUP = GQ_HEADS // GQ_KV_HEADS
CONV_CH = 256
CONV_WIDTH = 31
CONV_PAD = CONV_WIDTH // 2
ROPE_THETA = 10000.0
EPS = 1e-6
NEG_INF = -1e30
NA_W = NA_HEADS * HEAD_DIM
GQ_W = GQ_HEADS * HEAD_DIM
KV_W = GQ_KV_HEADS * HEAD_DIM
N_MOD = 9
RPB_ROWS = 2 * NA_WIN_ROWS - 1
RPB_COLS = 2 * NA_WIN_COLS - 1

LANES = 128
SUBLANES = 8
NA_BLOCK_ROWS = 4
NA_BLOCK = NA_BLOCK_ROWS * GRID_W
NA_SLOTS = 3 * NA_BLOCK_ROWS
GQA_KEY_CHUNK = 1024
CONV_HALO = 16
VMEM_LIMIT = 56 * 1024 * 1024


def _tile(n, target, align=8):
    if n <= target:
        return n
    for t in range(target - target % align, 0, -align):
        if n % t == 0:
            return t
    return n


def _params(sem):
    return pltpu.CompilerParams(dimension_semantics=sem, vmem_limit_bytes=VMEM_LIMIT)


def _sigmoid(x):
    return 1.0 / (1.0 + jnp.exp(-x))


def _mod_norm(x, g, shift, scale):
    ms = jnp.mean(x * x, axis=-1, keepdims=True)
    y = x * lax.rsqrt(ms + EPS) * g
    return y * (1.0 + scale) + shift


def _mod_kernel(c_ref, w_ref, b_ref, o_ref):
    c = c_ref[...]
    ca = (c * _sigmoid(c)).astype(BF16)
    o_ref[0] = jnp.dot(ca, w_ref[0].astype(BF16), preferred_element_type=F32) + b_ref[0]


def _mod_all(c, w_mod, b_mod):
    depth, d, nd = w_mod.shape
    bp = c.shape[0]
    tn = _tile(nd, 1024, LANES)
    return pl.pallas_call(
        _mod_kernel,
        out_shape=jax.ShapeDtypeStruct((depth, bp, nd), F32),
        grid=(depth, nd // tn),
        in_specs=[
            pl.BlockSpec((bp, d), lambda l, n: (0, 0)),
            pl.BlockSpec((1, d, tn), lambda l, n: (l, 0, n)),
            pl.BlockSpec((1, 1, tn), lambda l, n: (l, 0, n)),
        ],
        out_specs=pl.BlockSpec((1, bp, tn), lambda l, n: (l, 0, n)),
        compiler_params=_params(("arbitrary", "arbitrary")),
        name="mod_vectors",
    )(c, w_mod, b_mod.reshape(depth, 1, nd))


def _swiglu_half_step(x, mod_ref, g_ref, wgu_ref, wd_ref, sub):
    dff = wd_ref.shape[0]
    h = _mod_norm(x, g_ref[...], mod_ref[0, 3 * sub:3 * sub + 1, :],
                  mod_ref[0, 3 * sub + 1:3 * sub + 2, :]).astype(BF16)
    gate = jnp.dot(h, wgu_ref[:, :dff], preferred_element_type=F32)
    up = jnp.dot(h, wgu_ref[:, dff:], preferred_element_type=F32)
    a = (gate * _sigmoid(gate) * up).astype(BF16)
    y = jnp.dot(a, wd_ref[...], preferred_element_type=F32)
    return x + (0.5 * mod_ref[0, 3 * sub + 2:3 * sub + 3, :]) * y


def _ffn_kernel(x_ref, mod_ref, g_ref, wgu_ref, wd_ref, o_ref, *, sub):
    o_ref[0] = _swiglu_half_step(x_ref[0], mod_ref, g_ref, wgu_ref, wd_ref, sub)


def _ffn(x, mod, g, w_gu, w_down, *, sub):
    b, l, d = x.shape
    dff = w_down.shape[0]
    tm = _tile(l, 512)
    resident = pl.Buffered(1)
    return pl.pallas_call(
        functools.partial(_ffn_kernel, sub=sub),
        out_shape=jax.ShapeDtypeStruct(x.shape, F32),
        grid=(b, l // tm),
        in_specs=[
            pl.BlockSpec((1, tm, d), lambda bi, i: (bi, i, 0)),
            pl.BlockSpec((1, N_MOD, d), lambda bi, i: (bi, 0, 0)),
            pl.BlockSpec((1, d), lambda bi, i: (0, 0)),
            pl.BlockSpec((d, 2 * dff), lambda bi, i: (0, 0), pipeline_mode=resident),
            pl.BlockSpec((dff, d), lambda bi, i: (0, 0), pipeline_mode=resident),
        ],
        out_specs=pl.BlockSpec((1, tm, d), lambda bi, i: (bi, i, 0)),
        compiler_params=_params(("parallel", "parallel")),
        name=f"ffn{sub}",
    )(x, mod, g.reshape(1, d), w_gu, w_down)


def _head_rms(z, gain, seg):
    sq = z * z
    hi = sq.astype(BF16)
    lo = (sq - hi.astype(F32)).astype(BF16)
    ssq = jnp.dot(hi, seg, preferred_element_type=F32) + jnp.dot(lo, seg, preferred_element_type=F32)
    return z * lax.rsqrt(ssq * (1.0 / HEAD_DIM) + EPS) * gain


def _rope(x, cos, sin_signed):
    lane = lax.broadcasted_iota(jnp.int32, x.shape, 1)
    partner = jnp.where(lane % 2 == 0, pltpu.roll(x, LANES - 1, 1), pltpu.roll(x, 1, 1))
    return x * cos + partner * sin_signed


def _inproj_kernel(x_ref, mod_ref, g_ref, w_ref, gqa_ref, gka_ref, gqb_ref, gkb_ref, cos_ref, sin_ref,
                   qa_ref, ka_ref, va_ref, qb_ref, kt_ref, vx_ref, u_ref):
    h = _mod_norm(x_ref[0], g_ref[...], mod_ref[0, 3:4, :], mod_ref[0, 4:5, :]).astype(BF16)
    z = jnp.dot(h, w_ref[...], preferred_element_type=F32)
    tm = z.shape[0]
    r = lax.broadcasted_iota(jnp.int32, (2 * LANES, 2 * LANES), 0) // HEAD_DIM
    c = lax.broadcasted_iota(jnp.int32, (2 * LANES, 2 * LANES), 1) // HEAD_DIM
    seg = (r == c).astype(BF16)
    cos = cos_ref[...]
    sin = sin_ref[...]
    scale = HEAD_DIM ** -0.5

    o = 0
    qa_ref[0] = (_head_rms(z[:, o:o + NA_W], gqa_ref[...], seg) * scale).astype(BF16)
    o += NA_W
    ka_ref[0] = _head_rms(z[:, o:o + NA_W], gka_ref[...], seg).astype(BF16)
    o += NA_W
    va_ref[0] = z[:, o:o + NA_W].astype(BF16)
    o += NA_W
    for cix in range(GQ_W // (2 * LANES)):
        qn = _head_rms(z[:, o:o + 2 * LANES], gqb_ref[...], seg)
        for half in range(2):
            qr = _rope(qn[:, half * LANES:(half + 1) * LANES], cos, sin)
            lo_ = cix * 2 * LANES + half * LANES
            qb_ref[0, :, lo_:lo_ + LANES] = (qr * scale).astype(BF16)
        o += 2 * LANES
    kn = _head_rms(z[:, o:o + KV_W], gkb_ref[...], seg[:KV_W, :KV_W])
    kr = _rope(kn, cos, sin)
    kt_ref[0] = kr.T.astype(BF16).reshape(GQ_KV_HEADS, HEAD_DIM, tm)
    o += KV_W
    vb = z[:, o:o + KV_W]
    lane = lax.broadcasted_iota(jnp.int32, vb.shape, 1)
    ones_col = jnp.where(lane == HEAD_DIM, 1.0, 0.0)
    vx_ref[0, 0] = jnp.where(lane < HEAD_DIM, vb, ones_col).astype(BF16)
    vx_ref[0, 1] = jnp.where(lane < HEAD_DIM, pltpu.roll(vb, HEAD_DIM, 1), ones_col).astype(BF16)
    o += KV_W
    ca = z[:, o:o + CONV_CH]
    cb = z[:, o + CONV_CH:o + 2 * CONV_CH]
    u_ref[0] = ca * _sigmoid(cb)


def _inproj(x, mod, g, w_in, gqa, gka, gqb, gkb, cos, sin):
    b, l, d = x.shape
    ncol = w_in.shape[1]
    tm = _tile(l, 512)
    tok = lambda w: pl.BlockSpec((1, tm, w), lambda bi, i: (bi, i, 0))
    vec = lambda w: pl.BlockSpec((1, w), lambda bi, i: (0, 0))
    return pl.pallas_call(
        _inproj_kernel,
        out_shape=(
            jax.ShapeDtypeStruct((b, l, NA_W), BF16),
            jax.ShapeDtypeStruct((b, l, NA_W), BF16),
            jax.ShapeDtypeStruct((b, l, NA_W), BF16),
            jax.ShapeDtypeStruct((b, l, GQ_W), BF16),
            jax.ShapeDtypeStruct((b, GQ_KV_HEADS, HEAD_DIM, l), BF16),
            jax.ShapeDtypeStruct((b, GQ_KV_HEADS, l, LANES), BF16),
            jax.ShapeDtypeStruct((b, l, CONV_CH), F32),
        ),
        grid=(b, l // tm),
        in_specs=[
            tok(d),
            pl.BlockSpec((1, N_MOD, d), lambda bi, i: (bi, 0, 0)),
            vec(d),
            pl.BlockSpec((d, ncol), lambda bi, i: (0, 0)),
            vec(NA_W), vec(NA_W), vec(2 * LANES), vec(KV_W),
            pl.BlockSpec((tm, LANES), lambda bi, i: (i, 0)),
            pl.BlockSpec((tm, LANES), lambda bi, i: (i, 0)),
        ],
        out_specs=(
            tok(NA_W), tok(NA_W), tok(NA_W), tok(GQ_W),
            pl.BlockSpec((1, GQ_KV_HEADS, HEAD_DIM, tm), lambda bi, i: (bi, 0, 0, i)),
            pl.BlockSpec((1, GQ_KV_HEADS, tm, LANES), lambda bi, i: (bi, 0, i, 0)),
            tok(CONV_CH),
        ),
        compiler_params=_params(("parallel", "parallel")),
        name="inproj",
    )(x, mod, g.reshape(1, d), w_in, gqa, gka, gqb, gkb, cos, sin)


def _na_kernel(q_ref, kp_ref, kc_ref, kn_ref, vp_ref, vc_ref, vn_ref, bias_ref, o_ref, *, n):
    i = pl.program_id(1)
    nt = (((1,), (1,)), ((), ()))
    lane = lax.broadcasted_iota(jnp.int32, (NA_BLOCK, LANES), 1)
    for u in range(2):
        sb = 2 * i + u
        case = jnp.where(sb == 0, 0, jnp.where(sb == n - 1, 2, 1))
        ka_ref, kb_ref, va_ref, vb_ref = (kp_ref, kc_ref, vp_ref, vc_ref) if u == 0 else (kc_ref, kn_ref, vc_ref, vn_ref)
        wa = ka_ref.shape[1]
        rows = slice(u * NA_BLOCK, (u + 1) * NA_BLOCK)
        for hp in range(NA_HEADS // 2):
            cols = slice(hp * LANES, (hp + 1) * LANES)
            q2 = q_ref[0, rows, cols]
            ka, kb, va, vb = ka_ref[0, :, cols], kb_ref[0, :, cols], va_ref[0, :, cols], vb_ref[0, :, cols]
            zero = jnp.zeros_like(q2)
            qm = jnp.concatenate([jnp.where(lane < HEAD_DIM, q2, zero), jnp.where(lane < HEAD_DIM, zero, q2)], axis=0)
            bias = bias_ref[case, 2 * hp:2 * hp + 2].reshape(2 * NA_BLOCK, NA_SLOTS * GRID_W)
            sa = lax.dot_general(qm, ka, nt, preferred_element_type=F32) + bias[:, :wa]
            sb_ = lax.dot_general(qm, kb, nt, preferred_element_type=F32) + bias[:, wa:]
            m = jnp.maximum(jnp.max(sa, axis=-1, keepdims=True), jnp.max(sb_, axis=-1, keepdims=True))
            pa = jnp.exp(sa - m)
            pb = jnp.exp(sb_ - m)
            den = jnp.sum(pa, axis=-1, keepdims=True) + jnp.sum(pb, axis=-1, keepdims=True)
            o = jnp.dot(pa.astype(BF16), va, preferred_element_type=F32)
            o += jnp.dot(pb.astype(BF16), vb, preferred_element_type=F32)
            o = o / den
            o_ref[0, rows, cols] = jnp.where(lane < HEAD_DIM, o[:NA_BLOCK], o[NA_BLOCK:])


def _na_bias(rpb):
    c = jnp.arange(GRID_W)
    cs = jnp.clip(c - NA_WIN_COLS // 2, 0, GRID_W - NA_WIN_COLS)
    col_off = jnp.clip(c[None, :] - c[:, None] + (NA_WIN_COLS - 1), 0, RPB_COLS - 1)
    col_ok = (c[None, :] >= cs[:, None]) & (c[None, :] < cs[:, None] + NA_WIN_COLS)
    toep = jnp.where(col_ok[None, None], rpb.astype(F32)[:, :, col_off], NEG_INF)
    a = jnp.arange(NA_BLOCK_ROWS)[:, None]
    s = jnp.arange(NA_SLOTS)[None, :]
    rel = s - NA_BLOCK_ROWS - a
    half = NA_WIN_ROWS // 2
    valid = jnp.stack([
        jnp.broadcast_to((s >= NA_BLOCK_ROWS) & (s < NA_BLOCK_ROWS + NA_WIN_ROWS), rel.shape),
        (rel >= -half) & (rel < NA_WIN_ROWS - half),
        jnp.broadcast_to(s < NA_WIN_ROWS, rel.shape),
    ])
    ro = jnp.clip(rel + (NA_WIN_ROWS - 1), 0, RPB_ROWS - 1)
    t = toep[:, ro]
    t = jnp.where(valid[:, None, :, :, None, None], t[None], NEG_INF)
    t = t.transpose(0, 1, 2, 4, 3, 5)
    return t.reshape(3, NA_HEADS, NA_BLOCK, NA_SLOTS * GRID_W)


def _na_attention(qa, ka, va, bias):
    b, l, w = qa.shape
    n = l // NA_BLOCK
    assert l % (2 * NA_BLOCK) == 0 and n >= 4, "sequence must cover an even number (>= 4) of query blocks"
    cur = pl.BlockSpec((1, 2 * NA_BLOCK, w), lambda bi, i: (bi, i, 0))
    prev = pl.BlockSpec((1, NA_BLOCK, w), lambda bi, i: (bi, jnp.maximum(2 * i - 1, 0), 0))
    nxt = pl.BlockSpec((1, NA_BLOCK, w), lambda bi, i: (bi, jnp.minimum(2 * i + 2, n - 1), 0))
    return pl.pallas_call(
        functools.partial(_na_kernel, n=n),
        out_shape=jax.ShapeDtypeStruct((b, l, w), F32),
        grid=(b, n // 2),
        in_specs=[cur, prev, cur, nxt, prev, cur, nxt,
                  pl.BlockSpec(bias.shape, lambda bi, i: (0, 0, 0, 0), pipeline_mode=pl.Buffered(1))],
        out_specs=cur,
        compiler_params=_params(("parallel", "parallel")),
        name="na_attention",
    )(qa, ka, ka, ka, va, va, va, bias)


def _gqa_kernel(q_ref, kt_ref, v_ref, o_ref, qs_ref, s_ref, m_ref, acc_ref, *, tq, tk, nk):
    for h in range(GQ_GROUP):
        qs_ref[h * tq:(h + 1) * tq, :] = q_ref[0, :, h * HEAD_DIM:(h + 1) * HEAD_DIM]
    m_ref[...] = jnp.full(m_ref.shape, NEG_INF, F32)
    acc_ref[...] = jnp.zeros_like(acc_ref)

    def scores(j, slot):
        k0 = pl.multiple_of(j * tk, tk)
        s_ref[slot] = jnp.dot(qs_ref[...], kt_ref[0, 0, :, pl.ds(k0, tk)], preferred_element_type=F32)

    def accumulate(j, slot):
        k0 = pl.multiple_of(j * tk, tk)
        s = s_ref[slot]
        m_prev = m_ref[...]
        m_new = jnp.maximum(m_prev, jnp.max(s, axis=-1, keepdims=True))
        alpha = jnp.exp(m_prev - m_new)
        p = jnp.exp(s - jnp.tile(m_new, (1, tk // LANES)))
        pv = jnp.dot(p.astype(BF16), v_ref[0, 0, pl.ds(k0, tk), :], preferred_element_type=F32)
        acc_ref[...] = alpha * acc_ref[...] + pv
        m_ref[...] = m_new

    scores(0, 0)

    def pair(jj, carry):
        j = 2 * jj
        scores(j + 1, 1)
        accumulate(j, 0)
        scores(j + 2, 0)
        accumulate(j + 1, 1)
        return carry

    lax.fori_loop(0, nk // 2 - 1, pair, 0)
    scores(nk - 1, 1)
    accumulate(nk - 2, 0)
    accumulate(nk - 1, 1)
    for h in range(GQ_GROUP):
        a = acc_ref[h * tq:(h + 1) * tq, :]
        o_ref[0, :, h * HEAD_DIM:(h + 1) * HEAD_DIM] = a[:, :HEAD_DIM] / a[:, HEAD_DIM:HEAD_DIM + 1]


def _gqa_attention(qb, kt, vx):
    b, l, w = qb.shape
    tq = _tile(l, 256)
    tk = _tile(l // 2, GQA_KEY_CHUNK, LANES)
    nk = l // tk
    assert nk % 2 == 0, "key chunks are consumed in pairs"
    gw = GQ_GROUP * HEAD_DIM
    rows = GQ_GROUP * tq
    return pl.pallas_call(
        functools.partial(_gqa_kernel, tq=tq, tk=tk, nk=nk),
        out_shape=jax.ShapeDtypeStruct((b, l, w), F32),
        grid=(b, GQ_KV_HEADS, l // tq),
        in_specs=[
            pl.BlockSpec((1, tq, gw), lambda bi, kv, i: (bi, i, kv)),
            pl.BlockSpec((1, 1, HEAD_DIM, l), lambda bi, kv, i: (bi, kv, 0, 0)),
            pl.BlockSpec((1, 1, l, LANES), lambda bi, kv, i: (bi, kv, 0, 0)),
        ],
        out_specs=pl.BlockSpec((1, tq, gw), lambda bi, kv, i: (bi, i, kv)),
        scratch_shapes=[pltpu.VMEM((rows, HEAD_DIM), BF16), pltpu.VMEM((2, rows, tk), F32),
                        pltpu.VMEM((rows, LANES), F32), pltpu.VMEM((rows, LANES), F32)],
        compiler_params=_params(("parallel", "parallel", "arbitrary")),
        name="gqa_attention",
    )(qb, kt, vx)


def _conv_kernel(u_ref, up_ref, un_ref, w_ref, b_ref, g_ref, beta_ref, o_ref, ext_ref, *, t, nt, rc):
    i = pl.program_id(1)
    ext_ref[0, 0:CONV_HALO, :] = jnp.where(i > 0, up_ref[0], 0.0)
    ext_ref[0, CONV_HALO:CONV_HALO + t, :] = u_ref[0]
    ext_ref[0, CONV_HALO + t:, :] = jnp.where(i < nt - 1, un_ref[0], 0.0)
    ext = ext_ref[0]
    for r in range(1, SUBLANES):
        ext_ref[r] = pltpu.roll(ext, ext.shape[0] - r, 0)
    base = CONV_HALO - CONV_PAD
    for c0 in range(0, t, rc):
        acc = jnp.zeros((rc, CONV_CH), F32)
        for k in range(CONV_WIDTH):
            r = (base + k) % SUBLANES
            a0 = c0 + base + k - r
            acc = acc + ext_ref[r, a0:a0 + rc, :] * w_ref[k:k + 1, :]
        y = acc + b_ref[...]
        mu = jnp.mean(y, axis=-1, keepdims=True)
        yc = y - mu
        var = jnp.mean(yc * yc, axis=-1, keepdims=True)
        yn = yc * lax.rsqrt(var + EPS) * g_ref[...] + beta_ref[...]
        o_ref[0, c0:c0 + rc, :] = yn * _sigmoid(yn)


def _conv(u, w, bias, ln_g, ln_b):
    b, l, ch = u.shape
    t = _tile(l, 512)
    nt = l // t
    rc = _tile(t, 64)
    hb = t // CONV_HALO
    nh = l // CONV_HALO
    vec = pl.BlockSpec((1, ch), lambda bi, i: (0, 0))
    return pl.pallas_call(
        functools.partial(_conv_kernel, t=t, nt=nt, rc=rc),
        out_shape=jax.ShapeDtypeStruct(u.shape, F32),
        grid=(b, nt),
        in_specs=[
            pl.BlockSpec((1, t, ch), lambda bi, i: (bi, i, 0)),
            pl.BlockSpec((1, CONV_HALO, ch), lambda bi, i: (bi, jnp.maximum(i * hb - 1, 0), 0)),
            pl.BlockSpec((1, CONV_HALO, ch), lambda bi, i: (bi, jnp.minimum((i + 1) * hb, nh - 1), 0)),
            pl.BlockSpec((CONV_WIDTH, ch), lambda bi, i: (0, 0)),
            vec, vec, vec,
        ],
        out_specs=pl.BlockSpec((1, t, ch), lambda bi, i: (bi, i, 0)),
        scratch_shapes=[pltpu.VMEM((SUBLANES, t + 2 * CONV_HALO, ch), F32)],
        compiler_params=_params(("parallel", "parallel")),
        name="conformer_conv",
    )(u, u, u, w, bias.reshape(1, ch), ln_g.reshape(1, ch), ln_b.reshape(1, ch))


def _group_rms(y, g):
    ms = jnp.mean(y * y, axis=-1, keepdims=True)
    return (y * lax.rsqrt(ms + EPS) * g).astype(BF16)


def _outproj_ffn_kernel(x_ref, ya_ref, yb_ref, yc_ref, mod_ref, og_ref, wo_ref, g_ref, wgu_ref, wd_ref, o_ref):
    y = jnp.dot(_group_rms(ya_ref[0], og_ref[:, :NA_W]), wo_ref[:NA_W, :], preferred_element_type=F32)
    y += jnp.dot(_group_rms(yb_ref[0], og_ref[:, NA_W:NA_W + GQ_W]), wo_ref[NA_W:NA_W + GQ_W, :],
                 preferred_element_type=F32)
    y += jnp.dot(_group_rms(yc_ref[0], og_ref[:, NA_W + GQ_W:]), wo_ref[NA_W + GQ_W:, :],
                 preferred_element_type=F32)
    x = x_ref[0] + mod_ref[0, 5:6, :] * y
    o_ref[0] = _swiglu_half_step(x, mod_ref, g_ref, wgu_ref, wd_ref, 2)


def _outproj_ffn(x, ya, yb, yc, mod, og, w_out, g, w_gu, w_down):
    b, l, d = x.shape
    mix = w_out.shape[0]
    dff = w_down.shape[0]
    tm = _tile(l, 512)
    tok = lambda w: pl.BlockSpec((1, tm, w), lambda bi, i: (bi, i, 0))
    resident = lambda shape: pl.BlockSpec(shape, lambda bi, i: (0, 0), pipeline_mode=pl.Buffered(1))
    return pl.pallas_call(
        _outproj_ffn_kernel,
        out_shape=jax.ShapeDtypeStruct(x.shape, F32),
        grid=(b, l // tm),
        in_specs=[
            tok(d), tok(NA_W), tok(GQ_W), tok(CONV_CH),
            pl.BlockSpec((1, N_MOD, d), lambda bi, i: (bi, 0, 0)),
            pl.BlockSpec((1, mix), lambda bi, i: (0, 0)),
            resident((mix, d)),
            pl.BlockSpec((1, d), lambda bi, i: (0, 0)),
            resident((d, 2 * dff)),
            resident((dff, d)),
        ],
        out_specs=tok(d),
        compiler_params=_params(("parallel", "parallel")),
        name="outproj_ffn2",
    )(x, ya, yb, yc, mod, og.reshape(1, mix), w_out, g.reshape(1, d), w_gu, w_down)


def _rope_tables(l):
    t = jnp.arange(l)
    row = (t // GRID_W).astype(F32)
    col = (t % GRID_W).astype(F32)
    half = HEAD_DIM // 2
    freqs = ROPE_THETA ** (-jnp.arange(0, half, 2, dtype=F32) / half)
    ang = jnp.concatenate([row[:, None] * freqs, col[:, None] * freqs], axis=-1)
    cos = jnp.repeat(jnp.cos(ang), 2, axis=-1)
    sin = jnp.repeat(jnp.sin(ang), 2, axis=-1)
    sign = jnp.where(jnp.arange(HEAD_DIM) % 2 == 0, -1.0, 1.0).astype(F32)
    reps = LANES // HEAD_DIM
    return jnp.tile(cos, (1, reps)), jnp.tile(sin * sign, (1, reps))


def _trunk(x, mods, layers):
    l = x.shape[1]
    cos, sin = _rope_tables(l)
    for mod, p in zip(mods, layers):
        x = _ffn(x, mod, p["norm_ffn1_g"], p["ffn1_w_gu"], p["ffn1_w_down"], sub=0)
        qa, ka, va, qb, kt, vx, u = _inproj(x, mod, p["norm_mix_g"], p["w_in"], p["gqa"], p["gka"],
                                            p["gqb"], p["gkb"], cos, sin)
        ya = _na_attention(qa, ka, va, p["na_bias"])
        yb = _gqa_attention(qb, kt, vx)
        yc = _conv(u, p["conv_w"], p["conv_b"], p["conv_ln_g"], p["conv_ln_b"])
        x = _outproj_ffn(x, ya, yb, yc, mod, p["out_norm_g"], p["w_out"], p["norm_ffn2_g"], p["ffn2_w_gu"],
                         p["ffn2_w_down"])
    return x


def kernel(x_prompt, x_sample, c_prompt, c_sample, w_mod, b_mod, norm_ffn1_g, ffn1_w_gu, ffn1_w_down,
           norm_mix_g, w_in, na_q_g, na_k_g, na_rpb, gq_q_g, gq_k_g, conv_w, conv_b, conv_ln_g, conv_ln_b,
           out_norm_g, w_out, norm_ffn2_g, ffn2_w_gu, ffn2_w_down):
    depth, d = norm_ffn1_g.shape
    bp = c_prompt.shape[0]
    bs = c_sample.shape[0]
    pad = (-(bp + bs)) % 8
    c_all = jnp.concatenate([c_prompt, c_sample, jnp.zeros((pad, d), F32)], axis=0)
    mod_all = _mod_all(c_all, w_mod, b_mod).reshape(depth, bp + bs + pad, N_MOD, d)

    tile_gain = lambda g, w: jnp.tile(g.astype(F32), w // HEAD_DIM).reshape(1, w)
    layers = []
    for i in range(depth):
        layers.append(dict(
            norm_ffn1_g=norm_ffn1_g[i], ffn1_w_gu=ffn1_w_gu[i].astype(BF16),
            ffn1_w_down=ffn1_w_down[i].astype(BF16), norm_mix_g=norm_mix_g[i], w_in=w_in[i].astype(BF16),
            gqa=tile_gain(na_q_g[i], NA_W), gka=tile_gain(na_k_g[i], NA_W),
            gqb=tile_gain(gq_q_g[i], 2 * LANES), gkb=tile_gain(gq_k_g[i], KV_W),
            na_bias=_na_bias(na_rpb[i]), conv_w=conv_w[i], conv_b=conv_b[i], conv_ln_g=conv_ln_g[i],
            conv_ln_b=conv_ln_b[i], out_norm_g=out_norm_g[i], w_out=w_out[i].astype(BF16),
            norm_ffn2_g=norm_ffn2_g[i], ffn2_w_gu=ffn2_w_gu[i].astype(BF16),
            ffn2_w_down=ffn2_w_down[i].astype(BF16)))
    y_prompt = _trunk(x_prompt, [mod_all[i, :bp] for i in range(depth)], layers)
    y_sample = _trunk(x_sample, [mod_all[i, bp:bp + bs] for i in range(depth)], layers)
    return (y_prompt, y_sample)
```

```python
import functools

import jax
import jax.numpy as jnp
from jax import lax
from jax.experimental import pallas as pl
from jax.experimental.pallas import tpu as pltpu

F32 = jnp.float32
BF16 = jnp.bfloat16

GRID_W = 64
HEAD_DIM = 64
NA_HEADS = 4
NA_WIN_ROWS = 8
NA_WIN_COLS = 16
GQ_HEADS = 8
GQ_KV_HEADS = 2
GQ_GROUP = GQ_HEADS // GQ_KV_HEADS
CONV_CH = 256
CONV_WIDTH = 31
CONV_PAD = CONV_WIDTH // 2
ROPE_THETA = 10000.0
EPS = 1e-6
NEG_INF = -1e30
NA_W = NA_HEADS * HEAD_DIM
GQ_W = GQ_HEADS * HEAD_DIM
KV_W = GQ_KV_HEADS * HEAD_DIM
N_MOD = 9
RPB_ROWS = 2 * NA_WIN_ROWS - 1
RPB_COLS = 2 * NA_WIN_COLS - 1

LANES = 128
SUBLANES = 8
NA_BLOCK_ROWS = 4
NA_BLOCK = NA_BLOCK_ROWS * GRID_W
NA_SLOTS = 3 * NA_BLOCK_ROWS
GQA_KEY_CHUNK = 1024
GQA_TILES_PER_STEP = 8
CONV_HALO = 16
VMEM_LIMIT = 56 * 1024 * 1024


def _tile(n, target, align=8):
    if n <= target:
        return n
    for t in range(target - target % align, 0, -align):
        if n % t == 0:
            return t
    return n


def _params(sem):
    return pltpu.CompilerParams(dimension_semantics=sem, vmem_limit_bytes=VMEM_LIMIT)


def _sigmoid(x):
    return 1.0 / (1.0 + jnp.exp(-x))


def _mod_norm(x, g, shift, scale):
    ms = jnp.mean(x * x, axis=-1, keepdims=True)
    y = x * lax.rsqrt(ms + EPS) * g
    return y * (1.0 + scale) + shift


def _mod_kernel(c_ref, w_ref, b_ref, o_ref):
    c = c_ref[...]
    ca = (c * _sigmoid(c)).astype(BF16)
    o_ref[0] = jnp.dot(ca, w_ref[0].astype(BF16), preferred_element_type=F32) + b_ref[0]


def _mod_all(c, w_mod, b_mod):
    depth, d, nd = w_mod.shape
    bp = c.shape[0]
    tn = _tile(nd, 1024, LANES)
    return pl.pallas_call(
        _mod_kernel,
        out_shape=jax.ShapeDtypeStruct((depth, bp, nd), F32),
        grid=(depth, nd // tn),
        in_specs=[
            pl.BlockSpec((bp, d), lambda l, n: (0, 0)),
            pl.BlockSpec((1, d, tn), lambda l, n: (l, 0, n)),
            pl.BlockSpec((1, 1, tn), lambda l, n: (l, 0, n)),
        ],
        out_specs=pl.BlockSpec((1, bp, tn), lambda l, n: (l, 0, n)),
        compiler_params=_params(("arbitrary", "arbitrary")),
        name="mod_vectors",
    )(c, w_mod, b_mod.reshape(depth, 1, nd))


def _swiglu_half_step(x, mod_ref, g_ref, wgu_ref, wd_ref, sub):
    dff = wd_ref.shape[0]
    h = _mod_norm(x, g_ref[...], mod_ref[0, 3 * sub:3 * sub + 1, :],
                  mod_ref[0, 3 * sub + 1:3 * sub + 2, :]).astype(BF16)
    gate = jnp.dot(h, wgu_ref[:, :dff], preferred_element_type=F32)
    up = jnp.dot(h, wgu_ref[:, dff:], preferred_element_type=F32)
    a = (gate * _sigmoid(gate) * up).astype(BF16)
    y = jnp.dot(a, wd_ref[...], preferred_element_type=F32)
    return x + (0.5 * mod_ref[0, 3 * sub + 2:3 * sub + 3, :]) * y


def _ffn_kernel(x_ref, mod_ref, g_ref, wgu_ref, wd_ref, o_ref, *, sub):
    o_ref[0] = _swiglu_half_step(x_ref[0], mod_ref, g_ref, wgu_ref, wd_ref, sub)


def _ffn(x, mod, g, w_gu, w_down, *, sub):
    b, l, d = x.shape
    dff = w_down.shape[0]
    tm = _tile(l, 512)
    resident = pl.Buffered(1)
    return pl.pallas_call(
        functools.partial(_ffn_kernel, sub=sub),
        out_shape=jax.ShapeDtypeStruct(x.shape, F32),
        grid=(b, l // tm),
        in_specs=[
            pl.BlockSpec((1, tm, d), lambda bi, i: (bi, i, 0)),
            pl.BlockSpec((1, N_MOD, d), lambda bi, i: (bi, 0, 0)),
            pl.BlockSpec((1, d), lambda bi, i: (0, 0)),
            pl.BlockSpec((d, 2 * dff), lambda bi, i: (0, 0), pipeline_mode=resident),
            pl.BlockSpec((dff, d), lambda bi, i: (0, 0), pipeline_mode=resident),
        ],
        out_specs=pl.BlockSpec((1, tm, d), lambda bi, i: (bi, i, 0)),
        compiler_params=_params(("parallel", "parallel")),
        name=f"ffn{sub}",
    )(x, mod, g.reshape(1, d), w_gu, w_down)


def _head_rms(z, gain, seg):
    sq = z * z
    hi = sq.astype(BF16)
    lo = (sq - hi.astype(F32)).astype(BF16)
    ssq = jnp.dot(hi, seg, preferred_element_type=F32) + jnp.dot(lo, seg, preferred_element_type=F32)
    return z * lax.rsqrt(ssq * (1.0 / HEAD_DIM) + EPS) * gain


def _rope(x, cos, sin_signed):
    lane = lax.broadcasted_iota(jnp.int32, x.shape, 1)
    partner = jnp.where(lane % 2 == 0, pltpu.roll(x, LANES - 1, 1), pltpu.roll(x, 1, 1))
    return x * cos + partner * sin_signed


def _inproj_kernel(x_ref, mod_ref, g_ref, w_ref, gqa_ref, gka_ref, gqb_ref, gkb_ref, cos_ref, sin_ref,
                   qa_ref, ka_ref, va_ref, qb_ref, kt_ref, vx_ref, u_ref):
    h = _mod_norm(x_ref[0], g_ref[...], mod_ref[0, 3:4, :], mod_ref[0, 4:5, :]).astype(BF16)
    z = jnp.dot(h, w_ref[...], preferred_element_type=F32)
    tm = z.shape[0]
    r = lax.broadcasted_iota(jnp.int32, (2 * LANES, 2 * LANES), 0) // HEAD_DIM
    c = lax.broadcasted_iota(jnp.int32, (2 * LANES, 2 * LANES), 1) // HEAD_DIM
    seg = (r == c).astype(BF16)
    cos = cos_ref[...]
    sin = sin_ref[...]
    scale = HEAD_DIM ** -0.5

    o = 0
    qa_ref[0] = (_head_rms(z[:, o:o + NA_W], gqa_ref[...], seg) * scale).astype(BF16)
    o += NA_W
    ka_ref[0] = _head_rms(z[:, o:o + NA_W], gka_ref[...], seg).astype(BF16)
    o += NA_W
    va_ref[0] = z[:, o:o + NA_W].astype(BF16)
    o += NA_W
    for cix in range(GQ_W // (2 * LANES)):
        qn = _head_rms(z[:, o:o + 2 * LANES], gqb_ref[...], seg)
        for half in range(2):
            qr = _rope(qn[:, half * LANES:(half + 1) * LANES], cos, sin)
            lo_ = cix * 2 * LANES + half * LANES
            qb_ref[0, :, lo_:lo_ + LANES] = (qr * scale).astype(BF16)
        o += 2 * LANES
    kn = _head_rms(z[:, o:o + KV_W], gkb_ref[...], seg[:KV_W, :KV_W])
    kr = _rope(kn, cos, sin)
    kt_ref[0] = kr.T.astype(BF16).reshape(GQ_KV_HEADS, HEAD_DIM, tm)
    o += KV_W
    vb = z[:, o:o + KV_W]
    lane = lax.broadcasted_iota(jnp.int32, vb.shape, 1)
    ones_col = jnp.where(lane == HEAD_DIM, 1.0, 0.0)
    vx_ref[0, 0] = jnp.where(lane < HEAD_DIM, vb, ones_col).astype(BF16)
    vx_ref[0, 1] = jnp.where(lane < HEAD_DIM, pltpu.roll(vb, HEAD_DIM, 1), ones_col).astype(BF16)
    o += KV_W
    ca = z[:, o:o + CONV_CH]
    cb = z[:, o + CONV_CH:o + 2 * CONV_CH]
    u_ref[0] = ca * _sigmoid(cb)


def _inproj(x, mod, g, w_in, gqa, gka, gqb, gkb, cos, sin):
    b, l, d = x.shape
    ncol = w_in.shape[1]
    tm = _tile(l, 512)
    tok = lambda w: pl.BlockSpec((1, tm, w), lambda bi, i: (bi, i, 0))
    vec = lambda w: pl.BlockSpec((1, w), lambda bi, i: (0, 0))
    return pl.pallas_call(
        _inproj_kernel,
        out_shape=(
            jax.ShapeDtypeStruct((b, l, NA_W), BF16),
            jax.ShapeDtypeStruct((b, l, NA_W), BF16),
            jax.ShapeDtypeStruct((b, l, NA_W), BF16),
            jax.ShapeDtypeStruct((b, l, GQ_W), BF16),
            jax.ShapeDtypeStruct((b, GQ_KV_HEADS, HEAD_DIM, l), BF16),
            jax.ShapeDtypeStruct((b, GQ_KV_HEADS, l, LANES), BF16),
            jax.ShapeDtypeStruct((b, l, CONV_CH), F32),
        ),
        grid=(b, l // tm),
        in_specs=[
            tok(d),
            pl.BlockSpec((1, N_MOD, d), lambda bi, i: (bi, 0, 0)),
            vec(d),
            pl.BlockSpec((d, ncol), lambda bi, i: (0, 0)),
            vec(NA_W), vec(NA_W), vec(2 * LANES), vec(KV_W),
            pl.BlockSpec((tm, LANES), lambda bi, i: (i, 0)),
            pl.BlockSpec((tm, LANES), lambda bi, i: (i, 0)),
        ],
        out_specs=(
            tok(NA_W), tok(NA_W), tok(NA_W), tok(GQ_W),
            pl.BlockSpec((1, GQ_KV_HEADS, HEAD_DIM, tm), lambda bi, i: (bi, 0, 0, i)),
            pl.BlockSpec((1, GQ_KV_HEADS, tm, LANES), lambda bi, i: (bi, 0, i, 0)),
            tok(CONV_CH),
        ),
        compiler_params=_params(("parallel", "parallel")),
        name="inproj",
    )(x, mod, g.reshape(1, d), w_in, gqa, gka, gqb, gkb, cos, sin)


def _na_kernel(q_ref, kp_ref, kc_ref, kn_ref, vp_ref, vc_ref, vn_ref, bias_ref, o_ref, *, n):
    i = pl.program_id(1)
    nt = (((1,), (1,)), ((), ()))
    lane = lax.broadcasted_iota(jnp.int32, (NA_BLOCK, LANES), 1)
    for u in range(2):
        sb = 2 * i + u
        case = jnp.where(sb == 0, 0, jnp.where(sb == n - 1, 2, 1))
        ka_ref, kb_ref, va_ref, vb_ref = (kp_ref, kc_ref, vp_ref, vc_ref) if u == 0 else (kc_ref, kn_ref, vc_ref, vn_ref)
        wa = ka_ref.shape[1]
        rows = slice(u * NA_BLOCK, (u + 1) * NA_BLOCK)
        for hp in range(NA_HEADS // 2):
            cols = slice(hp * LANES, (hp + 1) * LANES)
            q2 = q_ref[0, rows, cols]
            ka, kb, va, vb = ka_ref[0, :, cols], kb_ref[0, :, cols], va_ref[0, :, cols], vb_ref[0, :, cols]
            zero = jnp.zeros_like(q2)
            qm = jnp.concatenate([jnp.where(lane < HEAD_DIM, q2, zero), jnp.where(lane < HEAD_DIM, zero, q2)], axis=0)
            bias = bias_ref[case, 2 * hp:2 * hp + 2].reshape(2 * NA_BLOCK, NA_SLOTS * GRID_W)
            sa = lax.dot_general(qm, ka, nt, preferred_element_type=F32) + bias[:, :wa]
            sb_ = lax.dot_general(qm, kb, nt, preferred_element_type=F32) + bias[:, wa:]
            m = jnp.maximum(jnp.max(sa, axis=-1, keepdims=True), jnp.max(sb_, axis=-1, keepdims=True))
            pa = jnp.exp(sa - m)
            pb = jnp.exp(sb_ - m)
            den = jnp.sum(pa, axis=-1, keepdims=True) + jnp.sum(pb, axis=-1, keepdims=True)
            o = jnp.dot(pa.astype(BF16), va, preferred_element_type=F32)
            o += jnp.dot(pb.astype(BF16), vb, preferred_element_type=F32)
            o = o / den
            o_ref[0, rows, cols] = jnp.where(lane < HEAD_DIM, o[:NA_BLOCK], o[NA_BLOCK:])


def _na_bias(rpb):
    c = jnp.arange(GRID_W)
    cs = jnp.clip(c - NA_WIN_COLS // 2, 0, GRID_W - NA_WIN_COLS)
    col_off = jnp.clip(c[None, :] - c[:, None] + (NA_WIN_COLS - 1), 0, RPB_COLS - 1)
    col_ok = (c[None, :] >= cs[:, None]) & (c[None, :] < cs[:, None] + NA_WIN_COLS)
    toep = jnp.where(col_ok[None, None], rpb.astype(F32)[:, :, col_off], NEG_INF)
    a = jnp.arange(NA_BLOCK_ROWS)[:, None]
    s = jnp.arange(NA_SLOTS)[None, :]
    rel = s - NA_BLOCK_ROWS - a
    half = NA_WIN_ROWS // 2
    valid = jnp.stack([
        jnp.broadcast_to((s >= NA_BLOCK_ROWS) & (s < NA_BLOCK_ROWS + NA_WIN_ROWS), rel.shape),
        (rel >= -half) & (rel < NA_WIN_ROWS - half),
        jnp.broadcast_to(s < NA_WIN_ROWS, rel.shape),
    ])
    ro = jnp.clip(rel + (NA_WIN_ROWS - 1), 0, RPB_ROWS - 1)
    t = toep[:, ro]
    t = jnp.where(valid[:, None, :, :, None, None], t[None], NEG_INF)
    t = t.transpose(0, 1, 2, 4, 3, 5)
    return t.reshape(3, NA_HEADS, NA_BLOCK, NA_SLOTS * GRID_W)


def _na_attention(qa, ka, va, bias):
    b, l, w = qa.shape
    n = l // NA_BLOCK
    assert l % (2 * NA_BLOCK) == 0 and n >= 4, "sequence must cover an even number (>= 4) of query blocks"
    cur = pl.BlockSpec((1, 2 * NA_BLOCK, w), lambda bi, i: (bi, i, 0))
    prev = pl.BlockSpec((1, NA_BLOCK, w), lambda bi, i: (bi, jnp.maximum(2 * i - 1, 0), 0))
    nxt = pl.BlockSpec((1, NA_BLOCK, w), lambda bi, i: (bi, jnp.minimum(2 * i + 2, n - 1), 0))
    return pl.pallas_call(
        functools.partial(_na_kernel, n=n),
        out_shape=jax.ShapeDtypeStruct((b, l, w), F32),
        grid=(b, n // 2),
        in_specs=[cur, prev, cur, nxt, prev, cur, nxt,
                  pl.BlockSpec(bias.shape, lambda bi, i: (0, 0, 0, 0), pipeline_mode=pl.Buffered(1))],
        out_specs=cur,
        compiler_params=_params(("parallel", "parallel")),
        name="na_attention",
    )(qa, ka, ka, ka, va, va, va, bias)


def _gqa_kernel(q_ref, kt_ref, v_ref, o_ref, qs_ref, s_ref, m_ref, acc_ref, *, tq, tk, nk, tiles):
    def row0(i):
        return i * tq if isinstance(i, int) else pl.multiple_of(i * tq, tq)

    def stack(i):
        for h in range(GQ_GROUP):
            qs_ref[i % 2, h * tq:(h + 1) * tq, :] = q_ref[0, pl.ds(row0(i), tq), h * HEAD_DIM:(h + 1) * HEAD_DIM]

    def scores(i, j, slot):
        k0 = pl.multiple_of(j * tk, tk)
        s_ref[slot] = jnp.dot(qs_ref[i % 2], kt_ref[0, 0, :, pl.ds(k0, tk)], preferred_element_type=F32)

    def accumulate(j, slot):
        k0 = pl.multiple_of(j * tk, tk)
        s = s_ref[slot]
        m_prev = m_ref[...]
        m_new = jnp.maximum(m_prev, jnp.max(s, axis=-1, keepdims=True))
        alpha = jnp.exp(m_prev - m_new)
        p = jnp.exp(s - jnp.tile(m_new, (1, tk // LANES)))
        pv = jnp.dot(p.astype(BF16), v_ref[0, 0, pl.ds(k0, tk), :], preferred_element_type=F32)
        acc_ref[...] = alpha * acc_ref[...] + pv
        m_ref[...] = m_new

    def finalize(i):
        for h in range(GQ_GROUP):
            a = acc_ref[h * tq:(h + 1) * tq, :]
            o_ref[0, pl.ds(row0(i), tq), h * HEAD_DIM:(h + 1) * HEAD_DIM] = (
                a[:, :HEAD_DIM] / a[:, HEAD_DIM:HEAD_DIM + 1])

    def tile(i, has_next):
        m_ref[...] = jnp.full(m_ref.shape, NEG_INF, F32)
        acc_ref[...] = jnp.zeros_like(acc_ref)

        def pair(jj, carry):
            j = 2 * jj
            scores(i, j + 1, 1)
            accumulate(j, 0)
            scores(i, j + 2, 0)
            accumulate(j + 1, 1)
            return carry

        lax.fori_loop(0, nk // 2 - 1, pair, 0)
        scores(i, nk - 1, 1)
        accumulate(nk - 2, 0)
        if has_next:
            stack(i + 1)
            scores(i + 1, 0, 0)
        accumulate(nk - 1, 1)
        finalize(i)

    stack(0)
    scores(0, 0, 0)

    def tile_with_next(i, carry):
        tile(i, True)
        return carry

    lax.fori_loop(0, tiles - 1, tile_with_next, 0)
    tile(tiles - 1, False)


def _gqa_attention(qb, kt, vx):
    b, l, w = qb.shape
    tq = _tile(l, 256)
    tk = _tile(l // 2, GQA_KEY_CHUNK, LANES)
    nk = l // tk
    assert nk % 2 == 0, "key chunks are consumed in pairs"
    tiles = _tile(l // tq, GQA_TILES_PER_STEP, 1)
    gw = GQ_GROUP * HEAD_DIM
    rows = GQ_GROUP * tq
    return pl.pallas_call(
        functools.partial(_gqa_kernel, tq=tq, tk=tk, nk=nk, tiles=tiles),
        out_shape=jax.ShapeDtypeStruct((b, l, w), F32),
        grid=(b, GQ_KV_HEADS, l // (tq * tiles)),
        in_specs=[
            pl.BlockSpec((1, tq * tiles, gw), lambda bi, kv, i: (bi, i, kv)),
            pl.BlockSpec((1, 1, HEAD_DIM, l), lambda bi, kv, i: (bi, kv, 0, 0)),
            pl.BlockSpec((1, 1, l, LANES), lambda bi, kv, i: (bi, kv, 0, 0)),
        ],
        out_specs=pl.BlockSpec((1, tq * tiles, gw), lambda bi, kv, i: (bi, i, kv)),
        scratch_shapes=[pltpu.VMEM((2, rows, HEAD_DIM), BF16), pltpu.VMEM((2, rows, tk), F32),
                        pltpu.VMEM((rows, LANES), F32), pltpu.VMEM((rows, LANES), F32)],
        compiler_params=_params(("parallel", "parallel", "arbitrary")),
        name="gqa_attention",
    )(qb, kt, vx)


def _conv_kernel(u_ref, up_ref, un_ref, w_ref, b_ref, g_ref, beta_ref, o_ref, ext_ref, *, t, nt, rc):
    i = pl.program_id(1)
    ext_ref[0, 0:CONV_HALO, :] = jnp.where(i > 0, up_ref[0], 0.0)
    ext_ref[0, CONV_HALO:CONV_HALO + t, :] = u_ref[0]
    ext_ref[0, CONV_HALO + t:, :] = jnp.where(i < nt - 1, un_ref[0], 0.0)
    ext = ext_ref[0]
    for r in range(1, SUBLANES):
        ext_ref[r] = pltpu.roll(ext, ext.shape[0] - r, 0)
    base = CONV_HALO - CONV_PAD
    for c0 in range(0, t, rc):
        acc = jnp.zeros((rc, CONV_CH), F32)
        for k in range(CONV_WIDTH):
            r = (base + k) % SUBLANES
            a0 = c0 + base + k - r
            acc = acc + ext_ref[r, a0:a0 + rc, :] * w_ref[k:k + 1, :]
        y = acc + b_ref[...]
        mu = jnp.mean(y, axis=-1, keepdims=True)
        yc = y - mu
        var = jnp.mean(yc * yc, axis=-1, keepdims=True)
        yn = yc * lax.rsqrt(var + EPS) * g_ref[...] + beta_ref[...]
        o_ref[0, c0:c0 + rc, :] = yn * _sigmoid(yn)


def _conv(u, w, bias, ln_g, ln_b):
    b, l, ch = u.shape
    t = _tile(l, 512)
    nt = l // t
    rc = _tile(t, 64)
    hb = t // CONV_HALO
    nh = l // CONV_HALO
    vec = pl.BlockSpec((1, ch), lambda bi, i: (0, 0))
    return pl.pallas_call(
        functools.partial(_conv_kernel, t=t, nt=nt, rc=rc),
        out_shape=jax.ShapeDtypeStruct(u.shape, F32),
        grid=(b, nt),
        in_specs=[
            pl.BlockSpec((1, t, ch), lambda bi, i: (bi, i, 0)),
            pl.BlockSpec((1, CONV_HALO, ch), lambda bi, i: (bi, jnp.maximum(i * hb - 1, 0), 0)),
            pl.BlockSpec((1, CONV_HALO, ch), lambda bi, i: (bi, jnp.minimum((i + 1) * hb, nh - 1), 0)),
            pl.BlockSpec((CONV_WIDTH, ch), lambda bi, i: (0, 0)),
            vec, vec, vec,
        ],
        out_specs=pl.BlockSpec((1, t, ch), lambda bi, i: (bi, i, 0)),
        scratch_shapes=[pltpu.VMEM((SUBLANES, t + 2 * CONV_HALO, ch), F32)],
        compiler_params=_params(("parallel", "parallel")),
        name="conformer_conv",
    )(u, u, u, w, bias.reshape(1, ch), ln_g.reshape(1, ch), ln_b.reshape(1, ch))


def _group_rms(y, g):
    ms = jnp.mean(y * y, axis=-1, keepdims=True)
    return (y * lax.rsqrt(ms + EPS) * g).astype(BF16)


def _outproj_ffn_kernel(x_ref, ya_ref, yb_ref, yc_ref, mod_ref, og_ref, wo_ref, g_ref, wgu_ref, wd_ref, o_ref):
    y = jnp.dot(_group_rms(ya_ref[0], og_ref[:, :NA_W]), wo_ref[:NA_W, :], preferred_element_type=F32)
    y += jnp.dot(_group_rms(yb_ref[0], og_ref[:, NA_W:NA_W + GQ_W]), wo_ref[NA_W:NA_W + GQ_W, :],
                 preferred_element_type=F32)
    y += jnp.dot(_group_rms(yc_ref[0], og_ref[:, NA_W + GQ_W:]), wo_ref[NA_W + GQ_W:, :],
                 preferred_element_type=F32)
    x = x_ref[0] + mod_ref[0, 5:6, :] * y
    o_ref[0] = _swiglu_half_step(x, mod_ref, g_ref, wgu_ref, wd_ref, 2)


def _outproj_ffn(x, ya, yb, yc, mod, og, w_out, g, w_gu, w_down):
    b, l, d = x.shape
    mix = w_out.shape[0]
    dff = w_down.shape[0]
    tm = _tile(l, 512)
    tok = lambda w: pl.BlockSpec((1, tm, w), lambda bi, i: (bi, i, 0))
    resident = lambda shape: pl.BlockSpec(shape, lambda bi, i: (0, 0), pipeline_mode=pl.Buffered(1))
    return pl.pallas_call(
        _outproj_ffn_kernel,
        out_shape=jax.ShapeDtypeStruct(x.shape, F32),
        grid=(b, l // tm),
        in_specs=[
            tok(d), tok(NA_W), tok(GQ_W), tok(CONV_CH),
            pl.BlockSpec((1, N_MOD, d), lambda bi, i: (bi, 0, 0)),
            pl.BlockSpec((1, mix), lambda bi, i: (0, 0)),
            resident((mix, d)),
            pl.BlockSpec((1, d), lambda bi, i: (0, 0)),
            resident((d, 2 * dff)),
            resident((dff, d)),
        ],
        out_specs=tok(d),
        compiler_params=_params(("parallel", "parallel")),
        name="outproj_ffn2",
    )(x, ya, yb, yc, mod, og.reshape(1, mix), w_out, g.reshape(1, d), w_gu, w_down)


def _rope_tables(l):
    t = jnp.arange(l)
    row = (t // GRID_W).astype(F32)
    col = (t % GRID_W).astype(F32)
    half = HEAD_DIM // 2
    freqs = ROPE_THETA ** (-jnp.arange(0, half, 2, dtype=F32) / half)
    ang = jnp.concatenate([row[:, None] * freqs, col[:, None] * freqs], axis=-1)
    cos = jnp.repeat(jnp.cos(ang), 2, axis=-1)
    sin = jnp.repeat(jnp.sin(ang), 2, axis=-1)
    sign = jnp.where(jnp.arange(HEAD_DIM) % 2 == 0, -1.0, 1.0).astype(F32)
    reps = LANES // HEAD_DIM
    return jnp.tile(cos, (1, reps)), jnp.tile(sin * sign, (1, reps))


def _trunk(x, mods, layers):
    l = x.shape[1]
    cos, sin = _rope_tables(l)
    for mod, p in zip(mods, layers):
        x = _ffn(x, mod, p["norm_ffn1_g"], p["ffn1_w_gu"], p["ffn1_w_down"], sub=0)
        qa, ka, va, qb, kt, vx, u = _inproj(x, mod, p["norm_mix_g"], p["w_in"], p["gqa"], p["gka"],
                                            p["gqb"], p["gkb"], cos, sin)
        ya = _na_attention(qa, ka, va, p["na_bias"])
        yb = _gqa_attention(qb, kt, vx)
        yc = _conv(u, p["conv_w"], p["conv_b"], p["conv_ln_g"], p["conv_ln_b"])
        x = _outproj_ffn(x, ya, yb, yc, mod, p["out_norm_g"], p["w_out"], p["norm_ffn2_g"], p["ffn2_w_gu"],
                         p["ffn2_w_down"])
    return x


def kernel(x_prompt, x_sample, c_prompt, c_sample, w_mod, b_mod, norm_ffn1_g, ffn1_w_gu, ffn1_w_down,
           norm_mix_g, w_in, na_q_g, na_k_g, na_rpb, gq_q_g, gq_k_g, conv_w, conv_b, conv_ln_g, conv_ln_b,
           out_norm_g, w_out, norm_ffn2_g, ffn2_w_gu, ffn2_w_down):
    depth, d = norm_ffn1_g.shape
    bp = c_prompt.shape[0]
    bs = c_sample.shape[0]
    pad = (-(bp + bs)) % 8
    c_all = jnp.concatenate([c_prompt, c_sample, jnp.zeros((pad, d), F32)], axis=0)
    mod_all = _mod_all(c_all, w_mod, b_mod).reshape(depth, bp + bs + pad, N_MOD, d)

    tile_gain = lambda g, w: jnp.tile(g.astype(F32), w // HEAD_DIM).reshape(1, w)
    layers = []
    for i in range(depth):
        layers.append(dict(
            norm_ffn1_g=norm_ffn1_g[i], ffn1_w_gu=ffn1_w_gu[i].astype(BF16),
            ffn1_w_down=ffn1_w_down[i].astype(BF16), norm_mix_g=norm_mix_g[i], w_in=w_in[i].astype(BF16),
            gqa=tile_gain(na_q_g[i], NA_W), gka=tile_gain(na_k_g[i], NA_W),
            gqb=tile_gain(gq_q_g[i], 2 * LANES), gkb=tile_gain(gq_k_g[i], KV_W),
            na_bias=_na_bias(na_rpb[i]), conv_w=conv_w[i], conv_b=conv_b[i], conv_ln_g=conv_ln_g[i],
            conv_ln_b=conv_ln_b[i], out_norm_g=out_norm_g[i], w_out=w_out[i].astype(BF16),
            norm_ffn2_g=norm_ffn2_g[i], ffn2_w_gu=ffn2_w_gu[i].astype(BF16),
            ffn2_w_down=ffn2_w_down[i].astype(BF16)))
    y_prompt = _trunk(x_prompt, [mod_all[i, :bp] for i in range(depth)], layers)
    y_sample = _trunk(x_sample, [mod_all[i, bp:bp + bs] for i in range(depth)], layers)
    return (y_prompt, y_sample)
```

```python
import functools

import jax
import jax.numpy as jnp
from jax import lax
from jax.experimental import pallas as pl
from jax.experimental.pallas import tpu as pltpu

F32 = jnp.float32
BF16 = jnp.bfloat16

GRID_W = 64
HEAD_DIM = 64
NA_HEADS = 4
NA_WIN_ROWS = 8
NA_WIN_COLS = 16
GQ_HEADS = 8
GQ_KV_HEADS = 2
GQ_GROUP = GQ_HEADS // GQ_KV_HEADS
CONV_CH = 256
CONV_WIDTH = 31
CONV_PAD = CONV_WIDTH // 2
ROPE_THETA = 10000.0
EPS = 1e-6
NEG_INF = -1e30
LOG2E = 1.4426950408889634
NA_W = NA_HEADS * HEAD_DIM
GQ_W = GQ_HEADS * HEAD_DIM
KV_W = GQ_KV_HEADS * HEAD_DIM
N_MOD = 9
RPB_ROWS = 2 * NA_WIN_ROWS - 1
RPB_COLS = 2 * NA_WIN_COLS - 1

LANES = 128
SUBLANES = 8
NA_BLOCK_ROWS = 4
NA_BLOCK = NA_BLOCK_ROWS * GRID_W
NA_SLOTS = 3 * NA_BLOCK_ROWS
GQA_KEY_CHUNK = 1024
GQA_TILES_PER_STEP = 4
CONV_HALO = 16
VMEM_LIMIT = 56 * 1024 * 1024


def _tile(n, target, align=8):
    if n <= target:
        return n
    for t in range(target - target % align, 0, -align):
        if n % t == 0:
            return t
    return n


def _params(sem):
    return pltpu.CompilerParams(dimension_semantics=sem, vmem_limit_bytes=VMEM_LIMIT)


def _sigmoid(x):
    return 1.0 / (1.0 + jnp.exp(-x))


def _mod_norm(x, g, shift, scale):
    ms = jnp.mean(x * x, axis=-1, keepdims=True)
    y = x * lax.rsqrt(ms + EPS) * g
    return y * (1.0 + scale) + shift


def _mod_kernel(c_ref, w_ref, b_ref, o_ref):
    c = c_ref[...]
    ca = (c * _sigmoid(c)).astype(BF16)
    o_ref[0] = jnp.dot(ca, w_ref[0].astype(BF16), preferred_element_type=F32) + b_ref[0]


def _mod_all(c, w_mod, b_mod):
    depth, d, nd = w_mod.shape
    bp = c.shape[0]
    tn = _tile(nd, 1024, LANES)
    return pl.pallas_call(
        _mod_kernel,
        out_shape=jax.ShapeDtypeStruct((depth, bp, nd), F32),
        grid=(depth, nd // tn),
        in_specs=[
            pl.BlockSpec((bp, d), lambda l, n: (0, 0)),
            pl.BlockSpec((1, d, tn), lambda l, n: (l, 0, n)),
            pl.BlockSpec((1, 1, tn), lambda l, n: (l, 0, n)),
        ],
        out_specs=pl.BlockSpec((1, bp, tn), lambda l, n: (l, 0, n)),
        compiler_params=_params(("arbitrary", "arbitrary")),
        name="mod_vectors",
    )(c, w_mod, b_mod.reshape(depth, 1, nd))


def _swiglu_half_step(x, mod_ref, g_ref, wgu_ref, wd_ref, sub):
    dff = wd_ref.shape[0]
    h = _mod_norm(x, g_ref[...], mod_ref[0, 3 * sub:3 * sub + 1, :],
                  mod_ref[0, 3 * sub + 1:3 * sub + 2, :]).astype(BF16)
    gate = jnp.dot(h, wgu_ref[:, :dff], preferred_element_type=F32)
    up = jnp.dot(h, wgu_ref[:, dff:], preferred_element_type=F32)
    a = (gate * _sigmoid(gate) * up).astype(BF16)
    y = jnp.dot(a, wd_ref[...], preferred_element_type=F32)
    return x + (0.5 * mod_ref[0, 3 * sub + 2:3 * sub + 3, :]) * y


def _ffn_kernel(x_ref, mod_ref, g_ref, wgu_ref, wd_ref, o_ref, *, sub):
    o_ref[0] = _swiglu_half_step(x_ref[0], mod_ref, g_ref, wgu_ref, wd_ref, sub)


def _ffn(x, mod, g, w_gu, w_down, *, sub):
    b, l, d = x.shape
    dff = w_down.shape[0]
    tm = _tile(l, 512)
    resident = pl.Buffered(1)
    return pl.pallas_call(
        functools.partial(_ffn_kernel, sub=sub),
        out_shape=jax.ShapeDtypeStruct(x.shape, F32),
        grid=(b, l // tm),
        in_specs=[
            pl.BlockSpec((1, tm, d), lambda bi, i: (bi, i, 0)),
            pl.BlockSpec((1, N_MOD, d), lambda bi, i: (bi, 0, 0)),
            pl.BlockSpec((1, d), lambda bi, i: (0, 0)),
            pl.BlockSpec((d, 2 * dff), lambda bi, i: (0, 0), pipeline_mode=resident),
            pl.BlockSpec((dff, d), lambda bi, i: (0, 0), pipeline_mode=resident),
        ],
        out_specs=pl.BlockSpec((1, tm, d), lambda bi, i: (bi, i, 0)),
        compiler_params=_params(("parallel", "parallel")),
        name=f"ffn{sub}",
    )(x, mod, g.reshape(1, d), w_gu, w_down)


def _head_rms(z, gain, seg):
    sq = z * z
    hi = sq.astype(BF16)
    lo = (sq - hi.astype(F32)).astype(BF16)
    ssq = jnp.dot(hi, seg, preferred_element_type=F32) + jnp.dot(lo, seg, preferred_element_type=F32)
    return z * lax.rsqrt(ssq * (1.0 / HEAD_DIM) + EPS) * gain


def _rope(x, cos, sin_signed):
    lane = lax.broadcasted_iota(jnp.int32, x.shape, 1)
    partner = jnp.where(lane % 2 == 0, pltpu.roll(x, LANES - 1, 1), pltpu.roll(x, 1, 1))
    return x * cos + partner * sin_signed


def _inproj_kernel(x_ref, mod_ref, g_ref, w_ref, gqa_ref, gka_ref, gqb_ref, gkb_ref, cos_ref, sin_ref,
                   qa_ref, ka_ref, va_ref, qb_ref, kt_ref, vx_ref, u_ref):
    h = _mod_norm(x_ref[0], g_ref[...], mod_ref[0, 3:4, :], mod_ref[0, 4:5, :]).astype(BF16)
    z = jnp.dot(h, w_ref[...], preferred_element_type=F32)
    tm = z.shape[0]
    r = lax.broadcasted_iota(jnp.int32, (2 * LANES, 2 * LANES), 0) // HEAD_DIM
    c = lax.broadcasted_iota(jnp.int32, (2 * LANES, 2 * LANES), 1) // HEAD_DIM
    seg = (r == c).astype(BF16)
    cos = cos_ref[...]
    sin = sin_ref[...]
    scale = HEAD_DIM ** -0.5 * LOG2E

    o = 0
    qa_ref[0] = (_head_rms(z[:, o:o + NA_W], gqa_ref[...], seg) * scale).astype(BF16)
    o += NA_W
    ka_ref[0] = _head_rms(z[:, o:o + NA_W], gka_ref[...], seg).astype(BF16)
    o += NA_W
    va_ref[0] = z[:, o:o + NA_W].astype(BF16)
    o += NA_W
    for cix in range(GQ_W // (2 * LANES)):
        qn = _head_rms(z[:, o:o + 2 * LANES], gqb_ref[...], seg)
        for half in range(2):
            qr = _rope(qn[:, half * LANES:(half + 1) * LANES], cos, sin)
            lo_ = cix * 2 * LANES + half * LANES
            qb_ref[0, :, lo_:lo_ + LANES] = (qr * scale).astype(BF16)
        o += 2 * LANES
    kn = _head_rms(z[:, o:o + KV_W], gkb_ref[...], seg[:KV_W, :KV_W])
    kr = _rope(kn, cos, sin)
    kt_ref[0] = kr.T.astype(BF16).reshape(GQ_KV_HEADS, HEAD_DIM, tm)
    o += KV_W
    vb = z[:, o:o + KV_W]
    lane = lax.broadcasted_iota(jnp.int32, vb.shape, 1)
    ones_col = jnp.where(lane == HEAD_DIM, 1.0, 0.0)
    vx_ref[0, 0] = jnp.where(lane < HEAD_DIM, vb, ones_col).astype(BF16)
    vx_ref[0, 1] = jnp.where(lane < HEAD_DIM, pltpu.roll(vb, HEAD_DIM, 1), ones_col).astype(BF16)
    o += KV_W
    ca = z[:, o:o + CONV_CH]
    cb = z[:, o + CONV_CH:o + 2 * CONV_CH]
    u_ref[0] = ca * _sigmoid(cb)


def _inproj(x, mod, g, w_in, gqa, gka, gqb, gkb, cos, sin):
    b, l, d = x.shape
    ncol = w_in.shape[1]
    tm = _tile(l, 512)
    tok = lambda w: pl.BlockSpec((1, tm, w), lambda bi, i: (bi, i, 0))
    vec = lambda w: pl.BlockSpec((1, w), lambda bi, i: (0, 0))
    return pl.pallas_call(
        _inproj_kernel,
        out_shape=(
            jax.ShapeDtypeStruct((b, l, NA_W), BF16),
            jax.ShapeDtypeStruct((b, l, NA_W), BF16),
            jax.ShapeDtypeStruct((b, l, NA_W), BF16),
            jax.ShapeDtypeStruct((b, l, GQ_W), BF16),
            jax.ShapeDtypeStruct((b, GQ_KV_HEADS, HEAD_DIM, l), BF16),
            jax.ShapeDtypeStruct((b, GQ_KV_HEADS, l, LANES), BF16),
            jax.ShapeDtypeStruct((b, l, CONV_CH), F32),
        ),
        grid=(b, l // tm),
        in_specs=[
            tok(d),
            pl.BlockSpec((1, N_MOD, d), lambda bi, i: (bi, 0, 0)),
            vec(d),
            pl.BlockSpec((d, ncol), lambda bi, i: (0, 0)),
            vec(NA_W), vec(NA_W), vec(2 * LANES), vec(KV_W),
            pl.BlockSpec((tm, LANES), lambda bi, i: (i, 0)),
            pl.BlockSpec((tm, LANES), lambda bi, i: (i, 0)),
        ],
        out_specs=(
            tok(NA_W), tok(NA_W), tok(NA_W), tok(GQ_W),
            pl.BlockSpec((1, GQ_KV_HEADS, HEAD_DIM, tm), lambda bi, i: (bi, 0, 0, i)),
            pl.BlockSpec((1, GQ_KV_HEADS, tm, LANES), lambda bi, i: (bi, 0, i, 0)),
            tok(CONV_CH),
        ),
        compiler_params=_params(("parallel", "parallel")),
        name="inproj",
    )(x, mod, g.reshape(1, d), w_in, gqa, gka, gqb, gkb, cos, sin)


def _na_kernel(q_ref, kp_ref, kc_ref, kn_ref, vp_ref, vc_ref, vn_ref, bias_ref, o_ref, *, n):
    i = pl.program_id(1)
    nt = (((1,), (1,)), ((), ()))
    lane = lax.broadcasted_iota(jnp.int32, (NA_BLOCK, LANES), 1)
    for u in range(2):
        sb = 2 * i + u
        case = jnp.where(sb == 0, 0, jnp.where(sb == n - 1, 2, 1))
        ka_ref, kb_ref, va_ref, vb_ref = (kp_ref, kc_ref, vp_ref, vc_ref) if u == 0 else (kc_ref, kn_ref, vc_ref, vn_ref)
        wa = ka_ref.shape[1]
        rows = slice(u * NA_BLOCK, (u + 1) * NA_BLOCK)
        for hp in range(NA_HEADS // 2):
            cols = slice(hp * LANES, (hp + 1) * LANES)
            q2 = q_ref[0, rows, cols]
            ka, kb, va, vb = ka_ref[0, :, cols], kb_ref[0, :, cols], va_ref[0, :, cols], vb_ref[0, :, cols]
            zero = jnp.zeros_like(q2)
            qm = jnp.concatenate([jnp.where(lane < HEAD_DIM, q2, zero), jnp.where(lane < HEAD_DIM, zero, q2)], axis=0)
            bias = bias_ref[case, 2 * hp:2 * hp + 2].reshape(2 * NA_BLOCK, NA_SLOTS * GRID_W)
            sa = lax.dot_general(qm, ka, nt, preferred_element_type=F32) + bias[:, :wa]
            sb_ = lax.dot_general(qm, kb, nt, preferred_element_type=F32) + bias[:, wa:]
            m = jnp.maximum(jnp.max(sa, axis=-1, keepdims=True), jnp.max(sb_, axis=-1, keepdims=True))
            pa = jnp.exp2(sa - m)
            pb = jnp.exp2(sb_ - m)
            den = jnp.sum(pa, axis=-1, keepdims=True) + jnp.sum(pb, axis=-1, keepdims=True)
            o = jnp.dot(pa.astype(BF16), va, preferred_element_type=F32)
            o += jnp.dot(pb.astype(BF16), vb, preferred_element_type=F32)
            o = o / den
            o_ref[0, rows, cols] = jnp.where(lane < HEAD_DIM, o[:NA_BLOCK], o[NA_BLOCK:])


def _na_bias(rpb):
    c = jnp.arange(GRID_W)
    cs = jnp.clip(c - NA_WIN_COLS // 2, 0, GRID_W - NA_WIN_COLS)
    col_off = jnp.clip(c[None, :] - c[:, None] + (NA_WIN_COLS - 1), 0, RPB_COLS - 1)
    col_ok = (c[None, :] >= cs[:, None]) & (c[None, :] < cs[:, None] + NA_WIN_COLS)
    toep = jnp.where(col_ok[None, None], rpb.astype(F32)[:, :, col_off] * LOG2E, NEG_INF)
    a = jnp.arange(NA_BLOCK_ROWS)[:, None]
    s = jnp.arange(NA_SLOTS)[None, :]
    rel = s - NA_BLOCK_ROWS - a
    half = NA_WIN_ROWS // 2
    valid = jnp.stack([
        jnp.broadcast_to((s >= NA_BLOCK_ROWS) & (s < NA_BLOCK_ROWS + NA_WIN_ROWS), rel.shape),
        (rel >= -half) & (rel < NA_WIN_ROWS - half),
        jnp.broadcast_to(s < NA_WIN_ROWS, rel.shape),
    ])
    ro = jnp.clip(rel + (NA_WIN_ROWS - 1), 0, RPB_ROWS - 1)
    t = toep[:, ro]
    t = jnp.where(valid[:, None, :, :, None, None], t[None], NEG_INF)
    t = t.transpose(0, 1, 2, 4, 3, 5)
    return t.reshape(3, NA_HEADS, NA_BLOCK, NA_SLOTS * GRID_W)


def _na_attention(qa, ka, va, bias):
    b, l, w = qa.shape
    n = l // NA_BLOCK
    assert l % (2 * NA_BLOCK) == 0 and n >= 4, "sequence must cover an even number (>= 4) of query blocks"
    cur = pl.BlockSpec((1, 2 * NA_BLOCK, w), lambda bi, i: (bi, i, 0))
    prev = pl.BlockSpec((1, NA_BLOCK, w), lambda bi, i: (bi, jnp.maximum(2 * i - 1, 0), 0))
    nxt = pl.BlockSpec((1, NA_BLOCK, w), lambda bi, i: (bi, jnp.minimum(2 * i + 2, n - 1), 0))
    return pl.pallas_call(
        functools.partial(_na_kernel, n=n),
        out_shape=jax.ShapeDtypeStruct((b, l, w), F32),
        grid=(b, n // 2),
        in_specs=[cur, prev, cur, nxt, prev, cur, nxt,
                  pl.BlockSpec(bias.shape, lambda bi, i: (0, 0, 0, 0), pipeline_mode=pl.Buffered(1))],
        out_specs=cur,
        compiler_params=_params(("parallel", "parallel")),
        name="na_attention",
    )(qa, ka, ka, ka, va, va, va, bias)


def _gqa_kernel(q_ref, kt_ref, v_ref, o_ref, qs_ref, s_ref, m_ref, acc_ref, *, tq, tk, nk, tiles):
    def row0(i):
        return i * tq if isinstance(i, int) else pl.multiple_of(i * tq, tq)

    def stack(i):
        for h in range(GQ_GROUP):
            qs_ref[i % 2, h * tq:(h + 1) * tq, :] = q_ref[0, pl.ds(row0(i), tq), h * HEAD_DIM:(h + 1) * HEAD_DIM]

    def scores(i, j, slot):
        k0 = pl.multiple_of(j * tk, tk)
        s_ref[slot] = jnp.dot(qs_ref[i % 2], kt_ref[0, 0, :, pl.ds(k0, tk)], preferred_element_type=F32)

    def accumulate(j, slot):
        k0 = pl.multiple_of(j * tk, tk)
        s = s_ref[slot]
        m_prev = m_ref[...]
        m_new = jnp.maximum(m_prev, jnp.max(s, axis=-1, keepdims=True))
        alpha = jnp.exp2(m_prev - m_new)
        p = jnp.exp2(s - jnp.tile(m_new, (1, tk // LANES)))
        pv = jnp.dot(p.astype(BF16), v_ref[0, 0, pl.ds(k0, tk), :], preferred_element_type=F32)
        acc_ref[...] = alpha * acc_ref[...] + pv
        m_ref[...] = m_new

    def finalize(i):
        for h in range(GQ_GROUP):
            a = acc_ref[h * tq:(h + 1) * tq, :]
            o_ref[0, pl.ds(row0(i), tq), h * HEAD_DIM:(h + 1) * HEAD_DIM] = (
                a[:, :HEAD_DIM] / a[:, HEAD_DIM:HEAD_DIM + 1])

    def tile(i, has_next):
        m_ref[...] = jnp.full(m_ref.shape, NEG_INF, F32)
        acc_ref[...] = jnp.zeros_like(acc_ref)

        def pair(jj, carry):
            j = 2 * jj
            scores(i, j + 1, 1)
            accumulate(j, 0)
            scores(i, j + 2, 0)
            accumulate(j + 1, 1)
            return carry

        lax.fori_loop(0, nk // 2 - 1, pair, 0)
        scores(i, nk - 1, 1)
        accumulate(nk - 2, 0)
        if has_next:
            stack(i + 1)
            scores(i + 1, 0, 0)
        accumulate(nk - 1, 1)
        finalize(i)

    stack(0)
    scores(0, 0, 0)

    def tile_with_next(i, carry):
        tile(i, True)
        return carry

    lax.fori_loop(0, tiles - 1, tile_with_next, 0)
    tile(tiles - 1, False)


def _gqa_attention(qb, kt, vx):
    b, l, w = qb.shape
    tq = _tile(l, 512)
    tk = _tile(l // 2, GQA_KEY_CHUNK, LANES)
    nk = l // tk
    assert nk % 2 == 0, "key chunks are consumed in pairs"
    tiles = _tile(l // tq, GQA_TILES_PER_STEP, 1)
    gw = GQ_GROUP * HEAD_DIM
    rows = GQ_GROUP * tq
    return pl.pallas_call(
        functools.partial(_gqa_kernel, tq=tq, tk=tk, nk=nk, tiles=tiles),
        out_shape=jax.ShapeDtypeStruct((b, l, w), F32),
        grid=(b, GQ_KV_HEADS, l // (tq * tiles)),
        in_specs=[
            pl.BlockSpec((1, tq * tiles, gw), lambda bi, kv, i: (bi, i, kv)),
            pl.BlockSpec((1, 1, HEAD_DIM, l), lambda bi, kv, i: (bi, kv, 0, 0)),
            pl.BlockSpec((1, 1, l, LANES), lambda bi, kv, i: (bi, kv, 0, 0)),
        ],
        out_specs=pl.BlockSpec((1, tq * tiles, gw), lambda bi, kv, i: (bi, i, kv)),
        scratch_shapes=[pltpu.VMEM((2, rows, HEAD_DIM), BF16), pltpu.VMEM((2, rows, tk), F32),
                        pltpu.VMEM((rows, LANES), F32), pltpu.VMEM((rows, LANES), F32)],
        compiler_params=_params(("parallel", "parallel", "arbitrary")),
        name="gqa_attention",
    )(qb, kt, vx)


def _conv_kernel(u_ref, up_ref, un_ref, w_ref, b_ref, g_ref, beta_ref, o_ref, ext_ref, *, t, nt, rc):
    i = pl.program_id(1)
    ext_ref[0, 0:CONV_HALO, :] = jnp.where(i > 0, up_ref[0], 0.0)
    ext_ref[0, CONV_HALO:CONV_HALO + t, :] = u_ref[0]
    ext_ref[0, CONV_HALO + t:, :] = jnp.where(i < nt - 1, un_ref[0], 0.0)
    ext = ext_ref[0]
    for r in range(1, SUBLANES):
        ext_ref[r] = pltpu.roll(ext, ext.shape[0] - r, 0)
    base = CONV_HALO - CONV_PAD
    for c0 in range(0, t, rc):
        acc = jnp.zeros((rc, CONV_CH), F32)
        for k in range(CONV_WIDTH):
            r = (base + k) % SUBLANES
            a0 = c0 + base + k - r
            acc = acc + ext_ref[r, a0:a0 + rc, :] * w_ref[k:k + 1, :]
        y = acc + b_ref[...]
        mu = jnp.mean(y, axis=-1, keepdims=True)
        yc = y - mu
        var = jnp.mean(yc * yc, axis=-1, keepdims=True)
        yn = yc * lax.rsqrt(var + EPS) * g_ref[...] + beta_ref[...]
        o_ref[0, c0:c0 + rc, :] = yn * _sigmoid(yn)


def _conv(u, w, bias, ln_g, ln_b):
    b, l, ch = u.shape
    t = _tile(l, 512)
    nt = l // t
    rc = _tile(t, 64)
    hb = t // CONV_HALO
    nh = l // CONV_HALO
    vec = pl.BlockSpec((1, ch), lambda bi, i: (0, 0))
    return pl.pallas_call(
        functools.partial(_conv_kernel, t=t, nt=nt, rc=rc),
        out_shape=jax.ShapeDtypeStruct(u.shape, F32),
        grid=(b, nt),
        in_specs=[
            pl.BlockSpec((1, t, ch), lambda bi, i: (bi, i, 0)),
            pl.BlockSpec((1, CONV_HALO, ch), lambda bi, i: (bi, jnp.maximum(i * hb - 1, 0), 0)),
            pl.BlockSpec((1, CONV_HALO, ch), lambda bi, i: (bi, jnp.minimum((i + 1) * hb, nh - 1), 0)),
            pl.BlockSpec((CONV_WIDTH, ch), lambda bi, i: (0, 0)),
            vec, vec, vec,
        ],
        out_specs=pl.BlockSpec((1, t, ch), lambda bi, i: (bi, i, 0)),
        scratch_shapes=[pltpu.VMEM((SUBLANES, t + 2 * CONV_HALO, ch), F32)],
        compiler_params=_params(("parallel", "parallel")),
        name="conformer_conv",
    )(u, u, u, w, bias.reshape(1, ch), ln_g.reshape(1, ch), ln_b.reshape(1, ch))


def _group_rms(y, g):
    ms = jnp.mean(y * y, axis=-1, keepdims=True)
    return (y * lax.rsqrt(ms + EPS) * g).astype(BF16)


def _outproj_ffn_kernel(x_ref, ya_ref, yb_ref, yc_ref, mod_ref, og_ref, wo_ref, g_ref, wgu_ref, wd_ref, o_ref):
    y = jnp.dot(_group_rms(ya_ref[0], og_ref[:, :NA_W]), wo_ref[:NA_W, :], preferred_element_type=F32)
    y += jnp.dot(_group_rms(yb_ref[0], og_ref[:, NA_W:NA_W + GQ_W]), wo_ref[NA_W:NA_W + GQ_W, :],
                 preferred_element_type=F32)
    y += jnp.dot(_group_rms(yc_ref[0], og_ref[:, NA_W + GQ_W:]), wo_ref[NA_W + GQ_W:, :],
                 preferred_element_type=F32)
    x = x_ref[0] + mod_ref[0, 5:6, :] * y
    o_ref[0] = _swiglu_half_step(x, mod_ref, g_ref, wgu_ref, wd_ref, 2)


def _outproj_ffn(x, ya, yb, yc, mod, og, w_out, g, w_gu, w_down):
    b, l, d = x.shape
    mix = w_out.shape[0]
    dff = w_down.shape[0]
    tm = _tile(l, 512)
    tok = lambda w: pl.BlockSpec((1, tm, w), lambda bi, i: (bi, i, 0))
    resident = lambda shape: pl.BlockSpec(shape, lambda bi, i: (0, 0), pipeline_mode=pl.Buffered(1))
    return pl.pallas_call(
        _outproj_ffn_kernel,
        out_shape=jax.ShapeDtypeStruct(x.shape, F32),
        grid=(b, l // tm),
        in_specs=[
            tok(d), tok(NA_W), tok(GQ_W), tok(CONV_CH),
            pl.BlockSpec((1, N_MOD, d), lambda bi, i: (bi, 0, 0)),
            pl.BlockSpec((1, mix), lambda bi, i: (0, 0)),
            resident((mix, d)),
            pl.BlockSpec((1, d), lambda bi, i: (0, 0)),
            resident((d, 2 * dff)),
            resident((dff, d)),
        ],
        out_specs=tok(d),
        compiler_params=_params(("parallel", "parallel")),
        name="outproj_ffn2",
    )(x, ya, yb, yc, mod, og.reshape(1, mix), w_out, g.reshape(1, d), w_gu, w_down)


def _rope_tables(l):
    t = jnp.arange(l)
    row = (t // GRID_W).astype(F32)
    col = (t % GRID_W).astype(F32)
    half = HEAD_DIM // 2
    freqs = ROPE_THETA ** (-jnp.arange(0, half, 2, dtype=F32) / half)
    ang = jnp.concatenate([row[:, None] * freqs, col[:, None] * freqs], axis=-1)
    cos = jnp.repeat(jnp.cos(ang), 2, axis=-1)
    sin = jnp.repeat(jnp.sin(ang), 2, axis=-1)
    sign = jnp.where(jnp.arange(HEAD_DIM) % 2 == 0, -1.0, 1.0).astype(F32)
    reps = LANES // HEAD_DIM
    return jnp.tile(cos, (1, reps)), jnp.tile(sin * sign, (1, reps))


def _trunk(x, mods, layers):
    l = x.shape[1]
    cos, sin = _rope_tables(l)
    for mod, p in zip(mods, layers):
        x = _ffn(x, mod, p["norm_ffn1_g"], p["ffn1_w_gu"], p["ffn1_w_down"], sub=0)
        qa, ka, va, qb, kt, vx, u = _inproj(x, mod, p["norm_mix_g"], p["w_in"], p["gqa"], p["gka"],
                                            p["gqb"], p["gkb"], cos, sin)
        ya = _na_attention(qa, ka, va, p["na_bias"])
        yb = _gqa_attention(qb, kt, vx)
        yc = _conv(u, p["conv_w"], p["conv_b"], p["conv_ln_g"], p["conv_ln_b"])
        x = _outproj_ffn(x, ya, yb, yc, mod, p["out_norm_g"], p["w_out"], p["norm_ffn2_g"], p["ffn2_w_gu"],
                         p["ffn2_w_down"])
    return x


def kernel(x_prompt, x_sample, c_prompt, c_sample, w_mod, b_mod, norm_ffn1_g, ffn1_w_gu, ffn1_w_down,
           norm_mix_g, w_in, na_q_g, na_k_g, na_rpb, gq_q_g, gq_k_g, conv_w, conv_b, conv_ln_g, conv_ln_b,
           out_norm_g, w_out, norm_ffn2_g, ffn2_w_gu, ffn2_w_down):
    depth, d = norm_ffn1_g.shape
    bp = c_prompt.shape[0]
    bs = c_sample.shape[0]
    pad = (-(bp + bs)) % 8
    c_all = jnp.concatenate([c_prompt, c_sample, jnp.zeros((pad, d), F32)], axis=0)
    mod_all = _mod_all(c_all, w_mod, b_mod).reshape(depth, bp + bs + pad, N_MOD, d)

    tile_gain = lambda g, w: jnp.tile(g.astype(F32), w // HEAD_DIM).reshape(1, w)
    layers = []
    for i in range(depth):
        layers.append(dict(
            norm_ffn1_g=norm_ffn1_g[i], ffn1_w_gu=ffn1_w_gu[i].astype(BF16),
            ffn1_w_down=ffn1_w_down[i].astype(BF16), norm_mix_g=norm_mix_g[i], w_in=w_in[i].astype(BF16),
            gqa=tile_gain(na_q_g[i], NA_W), gka=tile_gain(na_k_g[i], NA_W),
            gqb=tile_gain(gq_q_g[i], 2 * LANES), gkb=tile_gain(gq_k_g[i], KV_W),
            na_bias=_na_bias(na_rpb[i]), conv_w=conv_w[i], conv_b=conv_b[i], conv_ln_g=conv_ln_g[i],
            conv_ln_b=conv_ln_b[i], out_norm_g=out_norm_g[i], w_out=w_out[i].astype(BF16),
            norm_ffn2_g=norm_ffn2_g[i], ffn2_w_gu=ffn2_w_gu[i].astype(BF16),
            ffn2_w_down=ffn2_w_down[i].astype(BF16)))
    y_prompt = _trunk(x_prompt, [mod_all[i, :bp] for i in range(depth)], layers)
    y_sample = _trunk(x_sample, [mod_all[i, bp:bp + bs] for i in range(depth)], layers)
    return (y_prompt, y_sample)
```

```python
import functools

import jax
import jax.numpy as jnp
from jax import lax
from jax.experimental import pallas as pl
from jax.experimental.pallas import tpu as pltpu

F32 = jnp.float32
BF16 = jnp.bfloat16

GRID_W = 64
HEAD_DIM = 64
NA_HEADS = 4
NA_WIN_ROWS = 8
NA_WIN_COLS = 16
GQ_HEADS = 8
GQ_KV_HEADS = 2
GQ_GROUP = GQ_HEADS // GQ_KV_HEADS
CONV_CH = 256
CONV_WIDTH = 31
CONV_PAD = CONV_WIDTH // 2
ROPE_THETA = 10000.0
EPS = 1e-6
NEG_INF = -1e30
LOG2E = 1.4426950408889634
NA_W = NA_HEADS * HEAD_DIM
GQ_W = GQ_HEADS * HEAD_DIM
KV_W = GQ_KV_HEADS * HEAD_DIM
N_MOD = 9
RPB_ROWS = 2 * NA_WIN_ROWS - 1
RPB_COLS = 2 * NA_WIN_COLS - 1

LANES = 128
SUBLANES = 8
NA_BLOCK_ROWS = 4
NA_BLOCK = NA_BLOCK_ROWS * GRID_W
NA_SLOTS = 3 * NA_BLOCK_ROWS
TOKEN_TILE = 512
GQA_QUERY_TILE = 512
GQA_KEY_CHUNK = 1024
GQA_TILES_PER_STEP = 4
CONV_ROW_CHUNK = 64
CONV_HALO = 16
VMEM_LIMIT = 56 * 1024 * 1024


def _tile(n, target, align=8):
    if n <= target:
        return n
    for t in range(target - target % align, 0, -align):
        if n % t == 0:
            return t
    return n


def _params(sem):
    return pltpu.CompilerParams(dimension_semantics=sem, vmem_limit_bytes=VMEM_LIMIT)


def _sigmoid(x):
    return 1.0 / (1.0 + jnp.exp(-x))


def _mod_norm(x, g, shift, scale):
    ms = jnp.mean(x * x, axis=-1, keepdims=True)
    y = x * lax.rsqrt(ms + EPS) * g
    return y * (1.0 + scale) + shift


def _mod_kernel(c_ref, w_ref, b_ref, o_ref):
    c = c_ref[...]
    ca = (c * _sigmoid(c)).astype(BF16)
    o_ref[0] = jnp.dot(ca, w_ref[0].astype(BF16), preferred_element_type=F32) + b_ref[0]


def _mod_all(c, w_mod, b_mod):
    depth, d, nd = w_mod.shape
    bp = c.shape[0]
    tn = _tile(nd, 1024, LANES)
    return pl.pallas_call(
        _mod_kernel,
        out_shape=jax.ShapeDtypeStruct((depth, bp, nd), F32),
        grid=(depth, nd // tn),
        in_specs=[
            pl.BlockSpec((bp, d), lambda l, n: (0, 0)),
            pl.BlockSpec((1, d, tn), lambda l, n: (l, 0, n)),
            pl.BlockSpec((1, 1, tn), lambda l, n: (l, 0, n)),
        ],
        out_specs=pl.BlockSpec((1, bp, tn), lambda l, n: (l, 0, n)),
        compiler_params=_params(("arbitrary", "arbitrary")),
        name="mod_vectors",
    )(c, w_mod, b_mod.reshape(depth, 1, nd))


def _swiglu_half_step(x, mod_ref, g_ref, wgu_ref, wd_ref, sub):
    dff = wd_ref.shape[0]
    h = _mod_norm(x, g_ref[...], mod_ref[0, 3 * sub:3 * sub + 1, :],
                  mod_ref[0, 3 * sub + 1:3 * sub + 2, :]).astype(BF16)
    gate = jnp.dot(h, wgu_ref[:, :dff], preferred_element_type=F32)
    up = jnp.dot(h, wgu_ref[:, dff:], preferred_element_type=F32)
    a = (gate * _sigmoid(gate) * up).astype(BF16)
    y = jnp.dot(a, wd_ref[...], preferred_element_type=F32)
    return x + (0.5 * mod_ref[0, 3 * sub + 2:3 * sub + 3, :]) * y


def _ffn_kernel(x_ref, mod_ref, g_ref, wgu_ref, wd_ref, o_ref, *, sub):
    o_ref[0] = _swiglu_half_step(x_ref[0], mod_ref, g_ref, wgu_ref, wd_ref, sub)


def _layer_weight(w, layer):
    return pl.BlockSpec((None,) + w.shape[1:], lambda bi, i: (layer, 0, 0), pipeline_mode=pl.Buffered(1))


def _ffn(x, mod, g, w_gu, w_down, *, sub, layer):
    b, l, d = x.shape
    tm = _tile(l, TOKEN_TILE)
    return pl.pallas_call(
        functools.partial(_ffn_kernel, sub=sub),
        out_shape=jax.ShapeDtypeStruct(x.shape, F32),
        grid=(b, l // tm),
        in_specs=[
            pl.BlockSpec((1, tm, d), lambda bi, i: (bi, i, 0)),
            pl.BlockSpec((1, N_MOD, d), lambda bi, i: (bi, 0, 0)),
            pl.BlockSpec((1, d), lambda bi, i: (0, 0)),
            _layer_weight(w_gu, layer),
            _layer_weight(w_down, layer),
        ],
        out_specs=pl.BlockSpec((1, tm, d), lambda bi, i: (bi, i, 0)),
        compiler_params=_params(("parallel", "parallel")),
        name=f"ffn{sub}",
    )(x, mod, g.reshape(1, d), w_gu, w_down)


def _head_rms(z, gain, seg):
    sq = z * z
    hi = sq.astype(BF16)
    lo = (sq - hi.astype(F32)).astype(BF16)
    ssq = jnp.dot(hi, seg, preferred_element_type=F32) + jnp.dot(lo, seg, preferred_element_type=F32)
    return z * lax.rsqrt(ssq * (1.0 / HEAD_DIM) + EPS) * gain


def _rope(x, cos, sin_signed):
    lane = lax.broadcasted_iota(jnp.int32, x.shape, 1)
    partner = jnp.where(lane % 2 == 0, pltpu.roll(x, LANES - 1, 1), pltpu.roll(x, 1, 1))
    return x * cos + partner * sin_signed


def _inproj_kernel(x_ref, mod_ref, g_ref, w_ref, gqa_ref, gka_ref, gqb_ref, gkb_ref, cos_ref, sin_ref,
                   qa_ref, ka_ref, va_ref, qb_ref, kt_ref, vx_ref, u_ref):
    h = _mod_norm(x_ref[0], g_ref[...], mod_ref[0, 3:4, :], mod_ref[0, 4:5, :]).astype(BF16)
    z = jnp.dot(h, w_ref[...], preferred_element_type=F32)
    tm = z.shape[0]
    r = lax.broadcasted_iota(jnp.int32, (2 * LANES, 2 * LANES), 0) // HEAD_DIM
    c = lax.broadcasted_iota(jnp.int32, (2 * LANES, 2 * LANES), 1) // HEAD_DIM
    seg = (r == c).astype(BF16)
    cos = cos_ref[...]
    sin = sin_ref[...]
    scale = HEAD_DIM ** -0.5 * LOG2E

    o = 0
    qa_ref[0] = (_head_rms(z[:, o:o + NA_W], gqa_ref[...], seg) * scale).astype(BF16)
    o += NA_W
    ka_ref[0] = _head_rms(z[:, o:o + NA_W], gka_ref[...], seg).astype(BF16)
    o += NA_W
    va_ref[0] = z[:, o:o + NA_W].astype(BF16)
    o += NA_W
    for cix in range(GQ_W // (2 * LANES)):
        qn = _head_rms(z[:, o:o + 2 * LANES], gqb_ref[...], seg)
        for half in range(2):
            qr = _rope(qn[:, half * LANES:(half + 1) * LANES], cos, sin)
            lo_ = cix * 2 * LANES + half * LANES
            qb_ref[0, :, lo_:lo_ + LANES] = (qr * scale).astype(BF16)
        o += 2 * LANES
    kn = _head_rms(z[:, o:o + KV_W], gkb_ref[...], seg[:KV_W, :KV_W])
    kr = _rope(kn, cos, sin)
    kt_ref[0] = kr.T.astype(BF16).reshape(GQ_KV_HEADS, HEAD_DIM, tm)
    o += KV_W
    vb = z[:, o:o + KV_W]
    lane = lax.broadcasted_iota(jnp.int32, vb.shape, 1)
    ones_col = jnp.where(lane == HEAD_DIM, 1.0, 0.0)
    vx_ref[0, 0] = jnp.where(lane < HEAD_DIM, vb, ones_col).astype(BF16)
    vx_ref[0, 1] = jnp.where(lane < HEAD_DIM, pltpu.roll(vb, HEAD_DIM, 1), ones_col).astype(BF16)
    o += KV_W
    ca = z[:, o:o + CONV_CH]
    cb = z[:, o + CONV_CH:o + 2 * CONV_CH]
    u_ref[0] = ca * _sigmoid(cb)


def _inproj(x, mod, g, w_in, gqa, gka, gqb, gkb, cos, sin, *, layer):
    b, l, d = x.shape
    tm = _tile(l, TOKEN_TILE)
    tok = lambda w: pl.BlockSpec((1, tm, w), lambda bi, i: (bi, i, 0))
    vec = lambda w: pl.BlockSpec((1, w), lambda bi, i: (0, 0))
    return pl.pallas_call(
        _inproj_kernel,
        out_shape=(
            jax.ShapeDtypeStruct((b, l, NA_W), BF16),
            jax.ShapeDtypeStruct((b, l, NA_W), BF16),
            jax.ShapeDtypeStruct((b, l, NA_W), BF16),
            jax.ShapeDtypeStruct((b, l, GQ_W), BF16),
            jax.ShapeDtypeStruct((b, GQ_KV_HEADS, HEAD_DIM, l), BF16),
            jax.ShapeDtypeStruct((b, GQ_KV_HEADS, l, LANES), BF16),
            jax.ShapeDtypeStruct((b, l, CONV_CH), F32),
        ),
        grid=(b, l // tm),
        in_specs=[
            tok(d),
            pl.BlockSpec((1, N_MOD, d), lambda bi, i: (bi, 0, 0)),
            vec(d),
            _layer_weight(w_in, layer),
            vec(NA_W), vec(NA_W), vec(2 * LANES), vec(KV_W),
            pl.BlockSpec((tm, LANES), lambda bi, i: (i, 0)),
            pl.BlockSpec((tm, LANES), lambda bi, i: (i, 0)),
        ],
        out_specs=(
            tok(NA_W), tok(NA_W), tok(NA_W), tok(GQ_W),
            pl.BlockSpec((1, GQ_KV_HEADS, HEAD_DIM, tm), lambda bi, i: (bi, 0, 0, i)),
            pl.BlockSpec((1, GQ_KV_HEADS, tm, LANES), lambda bi, i: (bi, 0, i, 0)),
            tok(CONV_CH),
        ),
        compiler_params=_params(("parallel", "parallel")),
        name="inproj",
    )(x, mod, g.reshape(1, d), w_in, gqa, gka, gqb, gkb, cos, sin)


def _na_kernel(q_ref, kp_ref, kc_ref, kn_ref, vp_ref, vc_ref, vn_ref, bias_ref, o_ref, *, n):
    i = pl.program_id(1)
    nt = (((1,), (1,)), ((), ()))
    lane = lax.broadcasted_iota(jnp.int32, (NA_BLOCK, LANES), 1)
    for u in range(2):
        sb = 2 * i + u
        case = jnp.where(sb == 0, 0, jnp.where(sb == n - 1, 2, 1))
        ka_ref, kb_ref, va_ref, vb_ref = (kp_ref, kc_ref, vp_ref, vc_ref) if u == 0 else (kc_ref, kn_ref, vc_ref, vn_ref)
        wa = ka_ref.shape[1]
        rows = slice(u * NA_BLOCK, (u + 1) * NA_BLOCK)
        for hp in range(NA_HEADS // 2):
            cols = slice(hp * LANES, (hp + 1) * LANES)
            q2 = q_ref[0, rows, cols]
            ka, kb, va, vb = ka_ref[0, :, cols], kb_ref[0, :, cols], va_ref[0, :, cols], vb_ref[0, :, cols]
            zero = jnp.zeros_like(q2)
            qm = jnp.concatenate([jnp.where(lane < HEAD_DIM, q2, zero), jnp.where(lane < HEAD_DIM, zero, q2)], axis=0)
            bias = bias_ref[case, 2 * hp:2 * hp + 2].reshape(2 * NA_BLOCK, NA_SLOTS * GRID_W)
            sa = lax.dot_general(qm, ka, nt, preferred_element_type=F32) + bias[:, :wa]
            sb_ = lax.dot_general(qm, kb, nt, preferred_element_type=F32) + bias[:, wa:]
            m = jnp.maximum(jnp.max(sa, axis=-1, keepdims=True), jnp.max(sb_, axis=-1, keepdims=True))
            pa = jnp.exp2(sa - m)
            pb = jnp.exp2(sb_ - m)
            den = jnp.sum(pa, axis=-1, keepdims=True) + jnp.sum(pb, axis=-1, keepdims=True)
            o = jnp.dot(pa.astype(BF16), va, preferred_element_type=F32)
            o += jnp.dot(pb.astype(BF16), vb, preferred_element_type=F32)
            o = o / den
            o_ref[0, rows, cols] = jnp.where(lane < HEAD_DIM, o[:NA_BLOCK], o[NA_BLOCK:])


def _na_bias(rpb):
    depth = rpb.shape[0]
    c = jnp.arange(GRID_W)
    cs = jnp.clip(c - NA_WIN_COLS // 2, 0, GRID_W - NA_WIN_COLS)
    col_off = jnp.clip(c[None, :] - c[:, None] + (NA_WIN_COLS - 1), 0, RPB_COLS - 1)
    col_ok = (c[None, :] >= cs[:, None]) & (c[None, :] < cs[:, None] + NA_WIN_COLS)
    pick_col = (col_off[None] == jnp.arange(RPB_COLS)[:, None, None]).astype(F32)
    a = jnp.arange(NA_BLOCK_ROWS)[:, None]
    s = jnp.arange(NA_SLOTS)[None, :]
    rel = s - NA_BLOCK_ROWS - a
    half = NA_WIN_ROWS // 2
    valid = jnp.stack([
        jnp.broadcast_to((s >= NA_BLOCK_ROWS) & (s < NA_BLOCK_ROWS + NA_WIN_ROWS), rel.shape),
        (rel >= -half) & (rel < NA_WIN_ROWS - half),
        jnp.broadcast_to(s < NA_WIN_ROWS, rel.shape),
    ])
    ro = jnp.clip(rel + (NA_WIN_ROWS - 1), 0, RPB_ROWS - 1)
    pick_row = (ro[..., None] == jnp.arange(RPB_ROWS)).astype(F32)
    t = jnp.einsum("lhrk,kqw,asr->lhaqsw", rpb.astype(F32) * LOG2E, pick_col, pick_row,
                   precision=lax.Precision.HIGHEST)
    ok = valid[:, :, None, :, None] & col_ok[None, None, :, None, :]
    t = jnp.where(ok[None, :, None], t[:, None], NEG_INF)
    return t.reshape(depth, 3, NA_HEADS, NA_BLOCK, NA_SLOTS * GRID_W)


def _na_attention(qa, ka, va, bias, *, layer):
    b, l, w = qa.shape
    n = l // NA_BLOCK
    assert l % (2 * NA_BLOCK) == 0 and n >= 4, "sequence must cover an even number (>= 4) of query blocks"
    cur = pl.BlockSpec((1, 2 * NA_BLOCK, w), lambda bi, i: (bi, i, 0))
    prev = pl.BlockSpec((1, NA_BLOCK, w), lambda bi, i: (bi, jnp.maximum(2 * i - 1, 0), 0))
    nxt = pl.BlockSpec((1, NA_BLOCK, w), lambda bi, i: (bi, jnp.minimum(2 * i + 2, n - 1), 0))
    return pl.pallas_call(
        functools.partial(_na_kernel, n=n),
        out_shape=jax.ShapeDtypeStruct((b, l, w), F32),
        grid=(b, n // 2),
        in_specs=[cur, prev, cur, nxt, prev, cur, nxt,
                  pl.BlockSpec((None,) + bias.shape[1:], lambda bi, i: (layer, 0, 0, 0, 0),
                               pipeline_mode=pl.Buffered(1))],
        out_specs=cur,
        compiler_params=_params(("parallel", "parallel")),
        name="na_attention",
    )(qa, ka, ka, ka, va, va, va, bias)


def _gqa_kernel(q_ref, kt_ref, v_ref, o_ref, qs_ref, s_ref, m_ref, acc_ref, *, tq, tk, nk, tiles):
    def row0(i):
        return i * tq if isinstance(i, int) else pl.multiple_of(i * tq, tq)

    def stack(i):
        for h in range(GQ_GROUP):
            qs_ref[i % 2, h * tq:(h + 1) * tq, :] = q_ref[0, pl.ds(row0(i), tq), h * HEAD_DIM:(h + 1) * HEAD_DIM]

    def scores(i, j, slot):
        k0 = pl.multiple_of(j * tk, tk)
        s_ref[slot] = jnp.dot(qs_ref[i % 2], kt_ref[0, 0, :, pl.ds(k0, tk)], preferred_element_type=F32)

    def accumulate(j, slot):
        k0 = pl.multiple_of(j * tk, tk)
        s = s_ref[slot]
        m_prev = m_ref[...]
        m_new = jnp.maximum(m_prev, jnp.max(s, axis=-1, keepdims=True))
        alpha = jnp.exp2(m_prev - m_new)
        p = jnp.exp2(s - jnp.tile(m_new, (1, tk // LANES)))
        pv = jnp.dot(p.astype(BF16), v_ref[0, 0, pl.ds(k0, tk), :], preferred_element_type=F32)
        acc_ref[...] = alpha * acc_ref[...] + pv
        m_ref[...] = m_new

    def finalize(i):
        for h in range(GQ_GROUP):
            a = acc_ref[h * tq:(h + 1) * tq, :]
            o_ref[0, pl.ds(row0(i), tq), h * HEAD_DIM:(h + 1) * HEAD_DIM] = (
                a[:, :HEAD_DIM] / a[:, HEAD_DIM:HEAD_DIM + 1])

    def tile(i, has_next):
        m_ref[...] = jnp.full(m_ref.shape, NEG_INF, F32)
        acc_ref[...] = jnp.zeros_like(acc_ref)

        def pair(jj, carry):
            j = 2 * jj
            scores(i, j + 1, 1)
            accumulate(j, 0)
            scores(i, j + 2, 0)
            accumulate(j + 1, 1)
            return carry

        lax.fori_loop(0, nk // 2 - 1, pair, 0)
        scores(i, nk - 1, 1)
        accumulate(nk - 2, 0)
        if has_next:
            stack(i + 1)
            scores(i + 1, 0, 0)
        accumulate(nk - 1, 1)
        finalize(i)

    stack(0)
    scores(0, 0, 0)

    def tile_with_next(i, carry):
        tile(i, True)
        return carry

    lax.fori_loop(0, tiles - 1, tile_with_next, 0)
    tile(tiles - 1, False)


def _gqa_attention(qb, kt, vx):
    b, l, w = qb.shape
    tq = _tile(l, GQA_QUERY_TILE)
    tk = _tile(l // 2, GQA_KEY_CHUNK, LANES)
    nk = l // tk
    assert nk % 2 == 0, "key chunks are consumed in pairs"
    tiles = _tile(l // tq, GQA_TILES_PER_STEP, 1)
    gw = GQ_GROUP * HEAD_DIM
    rows = GQ_GROUP * tq
    return pl.pallas_call(
        functools.partial(_gqa_kernel, tq=tq, tk=tk, nk=nk, tiles=tiles),
        out_shape=jax.ShapeDtypeStruct((b, l, w), F32),
        grid=(b, GQ_KV_HEADS, l // (tq * tiles)),
        in_specs=[
            pl.BlockSpec((1, tq * tiles, gw), lambda bi, kv, i: (bi, i, kv)),
            pl.BlockSpec((1, 1, HEAD_DIM, l), lambda bi, kv, i: (bi, kv, 0, 0)),
            pl.BlockSpec((1, 1, l, LANES), lambda bi, kv, i: (bi, kv, 0, 0)),
        ],
        out_specs=pl.BlockSpec((1, tq * tiles, gw), lambda bi, kv, i: (bi, i, kv)),
        scratch_shapes=[pltpu.VMEM((2, rows, HEAD_DIM), BF16), pltpu.VMEM((2, rows, tk), F32),
                        pltpu.VMEM((rows, LANES), F32), pltpu.VMEM((rows, LANES), F32)],
        compiler_params=_params(("parallel", "parallel", "arbitrary")),
        name="gqa_attention",
    )(qb, kt, vx)


def _conv_kernel(u_ref, up_ref, un_ref, w_ref, b_ref, g_ref, beta_ref, o_ref, ext_ref, *, t, nt, rc):
    i = pl.program_id(1)
    ext_ref[0, 0:CONV_HALO, :] = jnp.where(i > 0, up_ref[0], 0.0)
    ext_ref[0, CONV_HALO:CONV_HALO + t, :] = u_ref[0]
    ext_ref[0, CONV_HALO + t:, :] = jnp.where(i < nt - 1, un_ref[0], 0.0)
    ext = ext_ref[0]
    for r in range(1, SUBLANES):
        ext_ref[r] = pltpu.roll(ext, ext.shape[0] - r, 0)
    base = CONV_HALO - CONV_PAD
    for c0 in range(0, t, rc):
        acc = jnp.zeros((rc, CONV_CH), F32)
        for k in range(CONV_WIDTH):
            r = (base + k) % SUBLANES
            a0 = c0 + base + k - r
            acc = acc + ext_ref[r, a0:a0 + rc, :] * w_ref[k:k + 1, :]
        y = acc + b_ref[...]
        mu = jnp.mean(y, axis=-1, keepdims=True)
        yc = y - mu
        var = jnp.mean(yc * yc, axis=-1, keepdims=True)
        yn = yc * lax.rsqrt(var + EPS) * g_ref[...] + beta_ref[...]
        o_ref[0, c0:c0 + rc, :] = yn * _sigmoid(yn)


def _conv(u, w, bias, ln_g, ln_b):
    b, l, ch = u.shape
    t = _tile(l, TOKEN_TILE)
    nt = l // t
    rc = _tile(t, CONV_ROW_CHUNK)
    hb = t // CONV_HALO
    nh = l // CONV_HALO
    vec = pl.BlockSpec((1, ch), lambda bi, i: (0, 0))
    return pl.pallas_call(
        functools.partial(_conv_kernel, t=t, nt=nt, rc=rc),
        out_shape=jax.ShapeDtypeStruct(u.shape, F32),
        grid=(b, nt),
        in_specs=[
            pl.BlockSpec((1, t, ch), lambda bi, i: (bi, i, 0)),
            pl.BlockSpec((1, CONV_HALO, ch), lambda bi, i: (bi, jnp.maximum(i * hb - 1, 0), 0)),
            pl.BlockSpec((1, CONV_HALO, ch), lambda bi, i: (bi, jnp.minimum((i + 1) * hb, nh - 1), 0)),
            pl.BlockSpec((CONV_WIDTH, ch), lambda bi, i: (0, 0)),
            vec, vec, vec,
        ],
        out_specs=pl.BlockSpec((1, t, ch), lambda bi, i: (bi, i, 0)),
        scratch_shapes=[pltpu.VMEM((SUBLANES, t + 2 * CONV_HALO, ch), F32)],
        compiler_params=_params(("parallel", "parallel")),
        name="conformer_conv",
    )(u, u, u, w, bias.reshape(1, ch), ln_g.reshape(1, ch), ln_b.reshape(1, ch))


def _group_rms(y, g):
    ms = jnp.mean(y * y, axis=-1, keepdims=True)
    return (y * lax.rsqrt(ms + EPS) * g).astype(BF16)


def _outproj_ffn_kernel(x_ref, ya_ref, yb_ref, yc_ref, mod_ref, og_ref, wo_ref, g_ref, wgu_ref, wd_ref, o_ref):
    y = jnp.dot(_group_rms(ya_ref[0], og_ref[:, :NA_W]), wo_ref[:NA_W, :], preferred_element_type=F32)
    y += jnp.dot(_group_rms(yb_ref[0], og_ref[:, NA_W:NA_W + GQ_W]), wo_ref[NA_W:NA_W + GQ_W, :],
                 preferred_element_type=F32)
    y += jnp.dot(_group_rms(yc_ref[0], og_ref[:, NA_W + GQ_W:]), wo_ref[NA_W + GQ_W:, :],
                 preferred_element_type=F32)
    x = x_ref[0] + mod_ref[0, 5:6, :] * y
    o_ref[0] = _swiglu_half_step(x, mod_ref, g_ref, wgu_ref, wd_ref, 2)


def _outproj_ffn(x, ya, yb, yc, mod, og, w_out, g, w_gu, w_down, *, layer):
    b, l, d = x.shape
    mix = w_out.shape[1]
    tm = _tile(l, TOKEN_TILE)
    tok = lambda w: pl.BlockSpec((1, tm, w), lambda bi, i: (bi, i, 0))
    return pl.pallas_call(
        _outproj_ffn_kernel,
        out_shape=jax.ShapeDtypeStruct(x.shape, F32),
        grid=(b, l // tm),
        in_specs=[
            tok(d), tok(NA_W), tok(GQ_W), tok(CONV_CH),
            pl.BlockSpec((1, N_MOD, d), lambda bi, i: (bi, 0, 0)),
            pl.BlockSpec((1, mix), lambda bi, i: (0, 0)),
            _layer_weight(w_out, layer),
            pl.BlockSpec((1, d), lambda bi, i: (0, 0)),
            _layer_weight(w_gu, layer),
            _layer_weight(w_down, layer),
        ],
        out_specs=tok(d),
        compiler_params=_params(("parallel", "parallel")),
        name="outproj_ffn2",
    )(x, ya, yb, yc, mod, og.reshape(1, mix), w_out, g.reshape(1, d), w_gu, w_down)


def _rope_tables(l):
    t = jnp.arange(l)
    row = (t // GRID_W).astype(F32)
    col = (t % GRID_W).astype(F32)
    half = HEAD_DIM // 2
    freqs = ROPE_THETA ** (-jnp.arange(0, half, 2, dtype=F32) / half)
    ang = jnp.concatenate([row[:, None] * freqs, col[:, None] * freqs], axis=-1)
    cos = jnp.repeat(jnp.cos(ang), 2, axis=-1)
    sin = jnp.repeat(jnp.sin(ang), 2, axis=-1)
    sign = jnp.where(jnp.arange(HEAD_DIM) % 2 == 0, -1.0, 1.0).astype(F32)
    reps = LANES // HEAD_DIM
    return jnp.tile(cos, (1, reps)), jnp.tile(sin * sign, (1, reps))


def _trunk(x, mods, layers, big):
    l = x.shape[1]
    cos, sin = _rope_tables(l)
    for i, (mod, p) in enumerate(zip(mods, layers)):
        x = _ffn(x, mod, p["norm_ffn1_g"], big["ffn1_w_gu"], big["ffn1_w_down"], sub=0, layer=i)
        qa, ka, va, qb, kt, vx, u = _inproj(x, mod, p["norm_mix_g"], big["w_in"], p["gqa"], p["gka"],
                                            p["gqb"], p["gkb"], cos, sin, layer=i)
        ya = _na_attention(qa, ka, va, big["na_bias"], layer=i)
        yb = _gqa_attention(qb, kt, vx)
        yc = _conv(u, p["conv_w"], p["conv_b"], p["conv_ln_g"], p["conv_ln_b"])
        x = _outproj_ffn(x, ya, yb, yc, mod, p["out_norm_g"], big["w_out"], p["norm_ffn2_g"],
                         big["ffn2_w_gu"], big["ffn2_w_down"], layer=i)
    return x


def kernel(x_prompt, x_sample, c_prompt, c_sample, w_mod, b_mod, norm_ffn1_g, ffn1_w_gu, ffn1_w_down,
           norm_mix_g, w_in, na_q_g, na_k_g, na_rpb, gq_q_g, gq_k_g, conv_w, conv_b, conv_ln_g, conv_ln_b,
           out_norm_g, w_out, norm_ffn2_g, ffn2_w_gu, ffn2_w_down):
    depth, d = norm_ffn1_g.shape
    bp = c_prompt.shape[0]
    bs = c_sample.shape[0]
    pad = (-(bp + bs)) % 8
    c_all = jnp.concatenate([c_prompt, c_sample, jnp.zeros((pad, d), F32)], axis=0)
    mod_all = _mod_all(c_all, w_mod, b_mod).reshape(depth, bp + bs + pad, N_MOD, d)

    tile_gain = lambda g, w: jnp.tile(g.astype(F32), w // HEAD_DIM).reshape(1, w)
    big = dict(ffn1_w_gu=ffn1_w_gu.astype(BF16), ffn1_w_down=ffn1_w_down.astype(BF16), w_in=w_in.astype(BF16),
               w_out=w_out.astype(BF16), ffn2_w_gu=ffn2_w_gu.astype(BF16), ffn2_w_down=ffn2_w_down.astype(BF16),
               na_bias=_na_bias(na_rpb))
    layers = []
    for i in range(depth):
        layers.append(dict(
            norm_ffn1_g=norm_ffn1_g[i], norm_mix_g=norm_mix_g[i],
            gqa=tile_gain(na_q_g[i], NA_W), gka=tile_gain(na_k_g[i], NA_W),
            gqb=tile_gain(gq_q_g[i], 2 * LANES), gkb=tile_gain(gq_k_g[i], KV_W),
            conv_w=conv_w[i], conv_b=conv_b[i], conv_ln_g=conv_ln_g[i],
            conv_ln_b=conv_ln_b[i], out_norm_g=out_norm_g[i], norm_ffn2_g=norm_ffn2_g[i]))
    y_prompt = _trunk(x_prompt, [mod_all[i, :bp] for i in range(depth)], layers, big)
    y_sample = _trunk(x_sample, [mod_all[i, bp:bp + bs] for i in range(depth)], layers, big)
    return (y_prompt, y_sample)
```

```python
import functools

import jax
import jax.numpy as jnp
from jax import lax
from jax.experimental import pallas as pl
from jax.experimental.pallas import tpu as pltpu

F32 = jnp.float32
BF16 = jnp.bfloat16

GRID_W = 64
HEAD_DIM = 64
NA_HEADS = 4
NA_WIN_ROWS = 8
NA_WIN_COLS = 16
GQ_HEADS = 8
GQ_KV_HEADS = 2
GQ_GROUP = GQ_HEADS // GQ_KV_HEADS
CONV_CH = 256
CONV_WIDTH = 31
CONV_PAD = CONV_WIDTH // 2
ROPE_THETA = 10000.0
EPS = 1e-6
NEG_INF = -1e30
LOG2E = 1.4426950408889634
NA_W = NA_HEADS * HEAD_DIM
GQ_W = GQ_HEADS * HEAD_DIM
KV_W = GQ_KV_HEADS * HEAD_DIM
N_MOD = 9
RPB_ROWS = 2 * NA_WIN_ROWS - 1
RPB_COLS = 2 * NA_WIN_COLS - 1

LANES = 128
SUBLANES = 8
NA_BLOCK_ROWS = 4
NA_BLOCK = NA_BLOCK_ROWS * GRID_W
NA_SLOTS = 3 * NA_BLOCK_ROWS
NA_BLOCKS_PER_STEP = 8
TOKEN_TILE = 512
FFN_SUBTILES = 2
GQA_QUERY_TILE = 512
GQA_KEY_CHUNK = 1024
GQA_TILES_PER_STEP = 4
CONV_ROW_CHUNK = 64
CONV_HALO = 16
VMEM_LIMIT = 56 * 1024 * 1024


def _tile(n, target, align=8):
    if n <= target:
        return n
    for t in range(target - target % align, 0, -align):
        if n % t == 0:
            return t
    return n


def _params(sem):
    return pltpu.CompilerParams(dimension_semantics=sem, vmem_limit_bytes=VMEM_LIMIT)


def _sigmoid(x):
    return 1.0 / (1.0 + jnp.exp(-x))


def _mod_norm(x, g, shift, scale):
    ms = jnp.mean(x * x, axis=-1, keepdims=True)
    y = x * lax.rsqrt(ms + EPS) * g
    return y * (1.0 + scale) + shift


def _mod_kernel(c_ref, w_ref, b_ref, o_ref):
    c = c_ref[...]
    ca = (c * _sigmoid(c)).astype(BF16)
    o_ref[0] = jnp.dot(ca, w_ref[0].astype(BF16), preferred_element_type=F32) + b_ref[0]


def _mod_all(c, w_mod, b_mod):
    depth, d, nd = w_mod.shape
    bp = c.shape[0]
    tn = _tile(nd, 1024, LANES)
    return pl.pallas_call(
        _mod_kernel,
        out_shape=jax.ShapeDtypeStruct((depth, bp, nd), F32),
        grid=(depth, nd // tn),
        in_specs=[
            pl.BlockSpec((bp, d), lambda l, n: (0, 0)),
            pl.BlockSpec((1, d, tn), lambda l, n: (l, 0, n)),
            pl.BlockSpec((1, 1, tn), lambda l, n: (l, 0, n)),
        ],
        out_specs=pl.BlockSpec((1, bp, tn), lambda l, n: (l, 0, n)),
        compiler_params=_params(("arbitrary", "arbitrary")),
        name="mod_vectors",
    )(c, w_mod, b_mod.reshape(depth, 1, nd))


def _swiglu_half_step(x, mod_ref, g_ref, wgu_ref, wd_ref, sub):
    dff = wd_ref.shape[0]
    h = _mod_norm(x, g_ref[...], mod_ref[0, 3 * sub:3 * sub + 1, :],
                  mod_ref[0, 3 * sub + 1:3 * sub + 2, :]).astype(BF16)
    gate = jnp.dot(h, wgu_ref[:, :dff], preferred_element_type=F32)
    up = jnp.dot(h, wgu_ref[:, dff:], preferred_element_type=F32)
    a = (gate * _sigmoid(gate) * up).astype(BF16)
    y = jnp.dot(a, wd_ref[...], preferred_element_type=F32)
    return x + (0.5 * mod_ref[0, 3 * sub + 2:3 * sub + 3, :]) * y


def _ffn_kernel(x_ref, mod_ref, g_ref, wgu_ref, wd_ref, o_ref, *, sub):
    step = min(TOKEN_TILE, x_ref.shape[1])
    for r0 in range(0, x_ref.shape[1], step):
        rows = slice(r0, r0 + step)
        o_ref[0, rows, :] = _swiglu_half_step(x_ref[0, rows, :], mod_ref, g_ref, wgu_ref, wd_ref, sub)


def _layer_weight(w, layer):
    return pl.BlockSpec((None,) + w.shape[1:], lambda bi, i: (layer, 0, 0), pipeline_mode=pl.Buffered(1))


def _ffn(x, mod, g, w_gu, w_down, *, sub, layer):
    b, l, d = x.shape
    tm = _tile(l, FFN_SUBTILES * TOKEN_TILE, TOKEN_TILE)
    return pl.pallas_call(
        functools.partial(_ffn_kernel, sub=sub),
        out_shape=jax.ShapeDtypeStruct(x.shape, F32),
        grid=(b, l // tm),
        in_specs=[
            pl.BlockSpec((1, tm, d), lambda bi, i: (bi, i, 0)),
            pl.BlockSpec((1, N_MOD, d), lambda bi, i: (bi, 0, 0)),
            pl.BlockSpec((1, d), lambda bi, i: (0, 0)),
            _layer_weight(w_gu, layer),
            _layer_weight(w_down, layer),
        ],
        out_specs=pl.BlockSpec((1, tm, d), lambda bi, i: (bi, i, 0)),
        compiler_params=_params(("parallel", "parallel")),
        name=f"ffn{sub}",
    )(x, mod, g.reshape(1, d), w_gu, w_down)


def _head_rms(z, gain, seg):
    sq = z * z
    hi = sq.astype(BF16)
    lo = (sq - hi.astype(F32)).astype(BF16)
    ssq = jnp.dot(hi, seg, preferred_element_type=F32) + jnp.dot(lo, seg, preferred_element_type=F32)
    return z * lax.rsqrt(ssq * (1.0 / HEAD_DIM) + EPS) * gain


def _rope(x, cos, sin_signed):
    lane = lax.broadcasted_iota(jnp.int32, x.shape, 1)
    partner = jnp.where(lane % 2 == 0, pltpu.roll(x, LANES - 1, 1), pltpu.roll(x, 1, 1))
    return x * cos + partner * sin_signed


def _inproj_kernel(x_ref, mod_ref, g_ref, w_ref, gqa_ref, gka_ref, gqb_ref, gkb_ref, cos_ref, sin_ref,
                   qa_ref, ka_ref, va_ref, qb_ref, kt_ref, vx_ref, u_ref):
    r = lax.broadcasted_iota(jnp.int32, (2 * LANES, 2 * LANES), 0) // HEAD_DIM
    c = lax.broadcasted_iota(jnp.int32, (2 * LANES, 2 * LANES), 1) // HEAD_DIM
    seg = (r == c).astype(BF16)
    scale = HEAD_DIM ** -0.5 * LOG2E
    step = min(TOKEN_TILE, x_ref.shape[1])
    for r0 in range(0, x_ref.shape[1], step):
        rows = slice(r0, r0 + step)
        h = _mod_norm(x_ref[0, rows, :], g_ref[...], mod_ref[0, 3:4, :], mod_ref[0, 4:5, :]).astype(BF16)
        z = jnp.dot(h, w_ref[...], preferred_element_type=F32)
        cos = cos_ref[rows, :]
        sin = sin_ref[rows, :]

        o = 0
        qa_ref[0, rows, :] = (_head_rms(z[:, o:o + NA_W], gqa_ref[...], seg) * scale).astype(BF16)
        o += NA_W
        ka_ref[0, rows, :] = _head_rms(z[:, o:o + NA_W], gka_ref[...], seg).astype(BF16)
        o += NA_W
        va_ref[0, rows, :] = z[:, o:o + NA_W].astype(BF16)
        o += NA_W
        for cix in range(GQ_W // (2 * LANES)):
            qn = _head_rms(z[:, o:o + 2 * LANES], gqb_ref[...], seg)
            for half in range(2):
                qr = _rope(qn[:, half * LANES:(half + 1) * LANES], cos, sin)
                lo_ = cix * 2 * LANES + half * LANES
                qb_ref[0, rows, lo_:lo_ + LANES] = (qr * scale).astype(BF16)
            o += 2 * LANES
        kn = _head_rms(z[:, o:o + KV_W], gkb_ref[...], seg[:KV_W, :KV_W])
        kr = _rope(kn, cos, sin)
        kt_ref[0, :, :, rows] = kr.T.astype(BF16).reshape(GQ_KV_HEADS, HEAD_DIM, step)
        o += KV_W
        vb = z[:, o:o + KV_W]
        lane = lax.broadcasted_iota(jnp.int32, vb.shape, 1)
        ones_col = jnp.where(lane == HEAD_DIM, 1.0, 0.0)
        vx_ref[0, 0, rows, :] = jnp.where(lane < HEAD_DIM, vb, ones_col).astype(BF16)
        vx_ref[0, 1, rows, :] = jnp.where(lane < HEAD_DIM, pltpu.roll(vb, HEAD_DIM, 1), ones_col).astype(BF16)
        o += KV_W
        ca = z[:, o:o + CONV_CH]
        cb = z[:, o + CONV_CH:o + 2 * CONV_CH]
        u_ref[0, rows, :] = ca * _sigmoid(cb)


def _inproj(x, mod, g, w_in, gqa, gka, gqb, gkb, cos, sin, *, layer):
    b, l, d = x.shape
    tm = _tile(l, FFN_SUBTILES * TOKEN_TILE, TOKEN_TILE)
    tok = lambda w: pl.BlockSpec((1, tm, w), lambda bi, i: (bi, i, 0))
    vec = lambda w: pl.BlockSpec((1, w), lambda bi, i: (0, 0))
    return pl.pallas_call(
        _inproj_kernel,
        out_shape=(
            jax.ShapeDtypeStruct((b, l, NA_W), BF16),
            jax.ShapeDtypeStruct((b, l, NA_W), BF16),
            jax.ShapeDtypeStruct((b, l, NA_W), BF16),
            jax.ShapeDtypeStruct((b, l, GQ_W), BF16),
            jax.ShapeDtypeStruct((b, GQ_KV_HEADS, HEAD_DIM, l), BF16),
            jax.ShapeDtypeStruct((b, GQ_KV_HEADS, l, LANES), BF16),
            jax.ShapeDtypeStruct((b, l, CONV_CH), F32),
        ),
        grid=(b, l // tm),
        in_specs=[
            tok(d),
            pl.BlockSpec((1, N_MOD, d), lambda bi, i: (bi, 0, 0)),
            vec(d),
            _layer_weight(w_in, layer),
            vec(NA_W), vec(NA_W), vec(2 * LANES), vec(KV_W),
            pl.BlockSpec((tm, LANES), lambda bi, i: (i, 0)),
            pl.BlockSpec((tm, LANES), lambda bi, i: (i, 0)),
        ],
        out_specs=(
            tok(NA_W), tok(NA_W), tok(NA_W), tok(GQ_W),
            pl.BlockSpec((1, GQ_KV_HEADS, HEAD_DIM, tm), lambda bi, i: (bi, 0, 0, i)),
            pl.BlockSpec((1, GQ_KV_HEADS, tm, LANES), lambda bi, i: (bi, 0, i, 0)),
            tok(CONV_CH),
        ),
        compiler_params=_params(("parallel", "parallel")),
        name="inproj",
    )(x, mod, g.reshape(1, d), w_in, gqa, gka, gqb, gkb, cos, sin)


def _na_kernel(q_ref, kp_ref, kc_ref, kn_ref, vp_ref, vc_ref, vn_ref, bias_ref, o_ref, *, n, g):
    i = pl.program_id(1)
    nt = (((1,), (1,)), ((), ()))
    lane = lax.broadcasted_iota(jnp.int32, (NA_BLOCK, LANES), 1)
    for u in range(g):
        sb = g * i + u
        case = jnp.where(sb == 0, 0, jnp.where(sb == n - 1, 2, 1))
        if u == 0:
            parts = [(kp_ref, vp_ref, 0, NA_BLOCK), (kc_ref, vc_ref, 0, 2 * NA_BLOCK)]
        elif u == g - 1:
            parts = [(kc_ref, vc_ref, (g - 2) * NA_BLOCK, 2 * NA_BLOCK), (kn_ref, vn_ref, 0, NA_BLOCK)]
        else:
            parts = [(kc_ref, vc_ref, (u - 1) * NA_BLOCK, 3 * NA_BLOCK)]
        rows = slice(u * NA_BLOCK, (u + 1) * NA_BLOCK)
        for hp in range(NA_HEADS // 2):
            cols = slice(hp * LANES, (hp + 1) * LANES)
            q2 = q_ref[0, rows, cols]
            zero = jnp.zeros_like(q2)
            qm = jnp.concatenate([jnp.where(lane < HEAD_DIM, q2, zero), jnp.where(lane < HEAD_DIM, zero, q2)], axis=0)
            bias = bias_ref[case, 2 * hp:2 * hp + 2].reshape(2 * NA_BLOCK, NA_SLOTS * GRID_W)
            scores, col = [], 0
            for k_ref, _, r0, nr in parts:
                scores.append(lax.dot_general(qm, k_ref[0, r0:r0 + nr, cols], nt, preferred_element_type=F32)
                              + bias[:, col:col + nr])
                col += nr
            m = functools.reduce(jnp.maximum, [jnp.max(s, axis=-1, keepdims=True) for s in scores])
            probs = [jnp.exp2(s - m) for s in scores]
            den = sum(jnp.sum(p, axis=-1, keepdims=True) for p in probs)
            o = sum(jnp.dot(p.astype(BF16), v_ref[0, r0:r0 + nr, cols], preferred_element_type=F32)
                    for p, (_, v_ref, r0, nr) in zip(probs, parts))
            o = o / den
            o_ref[0, rows, cols] = jnp.where(lane < HEAD_DIM, o[:NA_BLOCK], o[NA_BLOCK:])


def _na_bias(rpb):
    depth = rpb.shape[0]
    c = jnp.arange(GRID_W)
    cs = jnp.clip(c - NA_WIN_COLS // 2, 0, GRID_W - NA_WIN_COLS)
    col_off = jnp.clip(c[None, :] - c[:, None] + (NA_WIN_COLS - 1), 0, RPB_COLS - 1)
    col_ok = (c[None, :] >= cs[:, None]) & (c[None, :] < cs[:, None] + NA_WIN_COLS)
    pick_col = (col_off[None] == jnp.arange(RPB_COLS)[:, None, None]).astype(F32)
    a = jnp.arange(NA_BLOCK_ROWS)[:, None]
    s = jnp.arange(NA_SLOTS)[None, :]
    rel = s - NA_BLOCK_ROWS - a
    half = NA_WIN_ROWS // 2
    valid = jnp.stack([
        jnp.broadcast_to((s >= NA_BLOCK_ROWS) & (s < NA_BLOCK_ROWS + NA_WIN_ROWS), rel.shape),
        (rel >= -half) & (rel < NA_WIN_ROWS - half),
        jnp.broadcast_to(s < NA_WIN_ROWS, rel.shape),
    ])
    ro = jnp.clip(rel + (NA_WIN_ROWS - 1), 0, RPB_ROWS - 1)
    pick_row = (ro[..., None] == jnp.arange(RPB_ROWS)).astype(F32)
    t = jnp.einsum("lhrk,kqw,asr->lhaqsw", rpb.astype(F32) * LOG2E, pick_col, pick_row,
                   precision=lax.Precision.HIGHEST)
    ok = valid[:, :, None, :, None] & col_ok[None, None, :, None, :]
    t = jnp.where(ok[None, :, None], t[:, None], NEG_INF)
    return t.reshape(depth, 3, NA_HEADS, NA_BLOCK, NA_SLOTS * GRID_W)


def _na_attention(qa, ka, va, bias, *, layer):
    b, l, w = qa.shape
    n = l // NA_BLOCK
    g = _tile(n, NA_BLOCKS_PER_STEP, 1)
    assert l % NA_BLOCK == 0 and g >= 2 and n >= 3, "sequence must split into steps of at least two query blocks"
    cur = pl.BlockSpec((1, g * NA_BLOCK, w), lambda bi, i: (bi, i, 0))
    prev = pl.BlockSpec((1, NA_BLOCK, w), lambda bi, i: (bi, jnp.maximum(g * i - 1, 0), 0))
    nxt = pl.BlockSpec((1, NA_BLOCK, w), lambda bi, i: (bi, jnp.minimum(g * (i + 1), n - 1), 0))
    return pl.pallas_call(
        functools.partial(_na_kernel, n=n, g=g),
        out_shape=jax.ShapeDtypeStruct((b, l, w), F32),
        grid=(b, n // g),
        in_specs=[cur, prev, cur, nxt, prev, cur, nxt,
                  pl.BlockSpec((None,) + bias.shape[1:], lambda bi, i: (layer, 0, 0, 0, 0),
                               pipeline_mode=pl.Buffered(1))],
        out_specs=cur,
        compiler_params=_params(("parallel", "parallel")),
        name="na_attention",
    )(qa, ka, ka, ka, va, va, va, bias)


def _gqa_kernel(q_ref, kt_ref, v_ref, o_ref, qs_ref, s_ref, m_ref, acc_ref, *, tq, tk, nk, tiles):
    def row0(i):
        return i * tq if isinstance(i, int) else pl.multiple_of(i * tq, tq)

    def stack(i):
        for h in range(GQ_GROUP):
            qs_ref[i % 2, h * tq:(h + 1) * tq, :] = q_ref[0, pl.ds(row0(i), tq), h * HEAD_DIM:(h + 1) * HEAD_DIM]

    def scores(i, j, slot):
        k0 = pl.multiple_of(j * tk, tk)
        s_ref[slot] = jnp.dot(qs_ref[i % 2], kt_ref[0, 0, :, pl.ds(k0, tk)], preferred_element_type=F32)

    def accumulate(j, slot):
        k0 = pl.multiple_of(j * tk, tk)
        s = s_ref[slot]
        m_prev = m_ref[...]
        m_new = jnp.maximum(m_prev, jnp.max(s, axis=-1, keepdims=True))
        alpha = jnp.exp2(m_prev - m_new)
        p = jnp.exp2(s - jnp.tile(m_new, (1, tk // LANES)))
        pv = jnp.dot(p.astype(BF16), v_ref[0, 0, pl.ds(k0, tk), :], preferred_element_type=F32)
        acc_ref[...] = alpha * acc_ref[...] + pv
        m_ref[...] = m_new

    def finalize(i):
        for h in range(GQ_GROUP):
            a = acc_ref[h * tq:(h + 1) * tq, :]
            o_ref[0, pl.ds(row0(i), tq), h * HEAD_DIM:(h + 1) * HEAD_DIM] = (
                a[:, :HEAD_DIM] / a[:, HEAD_DIM:HEAD_DIM + 1])

    def tile(i, has_next):
        m_ref[...] = jnp.full(m_ref.shape, NEG_INF, F32)
        acc_ref[...] = jnp.zeros_like(acc_ref)

        def pair(jj, carry):
            j = 2 * jj
            scores(i, j + 1, 1)
            accumulate(j, 0)
            scores(i, j + 2, 0)
            accumulate(j + 1, 1)
            return carry

        lax.fori_loop(0, nk // 2 - 1, pair, 0)
        scores(i, nk - 1, 1)
        accumulate(nk - 2, 0)
        if has_next:
            stack(i + 1)
            scores(i + 1, 0, 0)
        accumulate(nk - 1, 1)
        finalize(i)

    stack(0)
    scores(0, 0, 0)

    def tile_with_next(i, carry):
        tile(i, True)
        return carry

    lax.fori_loop(0, tiles - 1, tile_with_next, 0)
    tile(tiles - 1, False)


def _gqa_attention(qb, kt, vx):
    b, l, w = qb.shape
    tq = _tile(l, GQA_QUERY_TILE)
    tk = _tile(l // 2, GQA_KEY_CHUNK, LANES)
    nk = l // tk
    assert nk % 2 == 0, "key chunks are consumed in pairs"
    tiles = _tile(l // tq, GQA_TILES_PER_STEP, 1)
    gw = GQ_GROUP * HEAD_DIM
    rows = GQ_GROUP * tq
    return pl.pallas_call(
        functools.partial(_gqa_kernel, tq=tq, tk=tk, nk=nk, tiles=tiles),
        out_shape=jax.ShapeDtypeStruct((b, l, w), F32),
        grid=(b, GQ_KV_HEADS, l // (tq * tiles)),
        in_specs=[
            pl.BlockSpec((1, tq * tiles, gw), lambda bi, kv, i: (bi, i, kv)),
            pl.BlockSpec((1, 1, HEAD_DIM, l), lambda bi, kv, i: (bi, kv, 0, 0)),
            pl.BlockSpec((1, 1, l, LANES), lambda bi, kv, i: (bi, kv, 0, 0)),
        ],
        out_specs=pl.BlockSpec((1, tq * tiles, gw), lambda bi, kv, i: (bi, i, kv)),
        scratch_shapes=[pltpu.VMEM((2, rows, HEAD_DIM), BF16), pltpu.VMEM((2, rows, tk), F32),
                        pltpu.VMEM((rows, LANES), F32), pltpu.VMEM((rows, LANES), F32)],
        compiler_params=_params(("parallel", "parallel", "arbitrary")),
        name="gqa_attention",
    )(qb, kt, vx)


def _conv_kernel(u_ref, up_ref, un_ref, w_ref, b_ref, g_ref, beta_ref, o_ref, ext_ref, *, t, nt, rc):
    i = pl.program_id(1)
    ext_ref[0, 0:CONV_HALO, :] = jnp.where(i > 0, up_ref[0], 0.0)
    ext_ref[0, CONV_HALO:CONV_HALO + t, :] = u_ref[0]
    ext_ref[0, CONV_HALO + t:, :] = jnp.where(i < nt - 1, un_ref[0], 0.0)
    ext = ext_ref[0]
    for r in range(1, SUBLANES):
        ext_ref[r] = pltpu.roll(ext, ext.shape[0] - r, 0)
    base = CONV_HALO - CONV_PAD
    for c0 in range(0, t, rc):
        acc = jnp.zeros((rc, CONV_CH), F32)
        for k in range(CONV_WIDTH):
            r = (base + k) % SUBLANES
            a0 = c0 + base + k - r
            acc = acc + ext_ref[r, a0:a0 + rc, :] * w_ref[k:k + 1, :]
        y = acc + b_ref[...]
        mu = jnp.mean(y, axis=-1, keepdims=True)
        yc = y - mu
        var = jnp.mean(yc * yc, axis=-1, keepdims=True)
        yn = yc * lax.rsqrt(var + EPS) * g_ref[...] + beta_ref[...]
        o_ref[0, c0:c0 + rc, :] = yn * _sigmoid(yn)


def _conv(u, w, bias, ln_g, ln_b):
    b, l, ch = u.shape
    t = _tile(l, TOKEN_TILE)
    nt = l // t
    rc = _tile(t, CONV_ROW_CHUNK)
    hb = t // CONV_HALO
    nh = l // CONV_HALO
    vec = pl.BlockSpec((1, ch), lambda bi, i: (0, 0))
    return pl.pallas_call(
        functools.partial(_conv_kernel, t=t, nt=nt, rc=rc),
        out_shape=jax.ShapeDtypeStruct(u.shape, F32),
        grid=(b, nt),
        in_specs=[
            pl.BlockSpec((1, t, ch), lambda bi, i: (bi, i, 0)),
            pl.BlockSpec((1, CONV_HALO, ch), lambda bi, i: (bi, jnp.maximum(i * hb - 1, 0), 0)),
            pl.BlockSpec((1, CONV_HALO, ch), lambda bi, i: (bi, jnp.minimum((i + 1) * hb, nh - 1), 0)),
            pl.BlockSpec((CONV_WIDTH, ch), lambda bi, i: (0, 0)),
            vec, vec, vec,
        ],
        out_specs=pl.BlockSpec((1, t, ch), lambda bi, i: (bi, i, 0)),
        scratch_shapes=[pltpu.VMEM((SUBLANES, t + 2 * CONV_HALO, ch), F32)],
        compiler_params=_params(("parallel", "parallel")),
        name="conformer_conv",
    )(u, u, u, w, bias.reshape(1, ch), ln_g.reshape(1, ch), ln_b.reshape(1, ch))


def _group_rms(y, g):
    ms = jnp.mean(y * y, axis=-1, keepdims=True)
    return (y * lax.rsqrt(ms + EPS) * g).astype(BF16)


def _outproj_ffn_kernel(x_ref, ya_ref, yb_ref, yc_ref, mod_ref, og_ref, wo_ref, g_ref, wgu_ref, wd_ref, o_ref):
    step = min(TOKEN_TILE, x_ref.shape[1])
    for r0 in range(0, x_ref.shape[1], step):
        rows = slice(r0, r0 + step)
        y = jnp.dot(_group_rms(ya_ref[0, rows, :], og_ref[:, :NA_W]), wo_ref[:NA_W, :],
                    preferred_element_type=F32)
        y += jnp.dot(_group_rms(yb_ref[0, rows, :], og_ref[:, NA_W:NA_W + GQ_W]), wo_ref[NA_W:NA_W + GQ_W, :],
                     preferred_element_type=F32)
        y += jnp.dot(_group_rms(yc_ref[0, rows, :], og_ref[:, NA_W + GQ_W:]), wo_ref[NA_W + GQ_W:, :],
                     preferred_element_type=F32)
        x = x_ref[0, rows, :] + mod_ref[0, 5:6, :] * y
        o_ref[0, rows, :] = _swiglu_half_step(x, mod_ref, g_ref, wgu_ref, wd_ref, 2)


def _outproj_ffn(x, ya, yb, yc, mod, og, w_out, g, w_gu, w_down, *, layer):
    b, l, d = x.shape
    mix = w_out.shape[1]
    tm = _tile(l, FFN_SUBTILES * TOKEN_TILE, TOKEN_TILE)
    tok = lambda w: pl.BlockSpec((1, tm, w), lambda bi, i: (bi, i, 0))
    return pl.pallas_call(
        _outproj_ffn_kernel,
        out_shape=jax.ShapeDtypeStruct(x.shape, F32),
        grid=(b, l // tm),
        in_specs=[
            tok(d), tok(NA_W), tok(GQ_W), tok(CONV_CH),
            pl.BlockSpec((1, N_MOD, d), lambda bi, i: (bi, 0, 0)),
            pl.BlockSpec((1, mix), lambda bi, i: (0, 0)),
            _layer_weight(w_out, layer),
            pl.BlockSpec((1, d), lambda bi, i: (0, 0)),
            _layer_weight(w_gu, layer),
            _layer_weight(w_down, layer),
        ],
        out_specs=tok(d),
        compiler_params=_params(("parallel", "parallel")),
        name="outproj_ffn2",
    )(x, ya, yb, yc, mod, og.reshape(1, mix), w_out, g.reshape(1, d), w_gu, w_down)


def _rope_tables(l):
    t = jnp.arange(l)
    row = (t // GRID_W).astype(F32)
    col = (t % GRID_W).astype(F32)
    half = HEAD_DIM // 2
    freqs = ROPE_THETA ** (-jnp.arange(0, half, 2, dtype=F32) / half)
    ang = jnp.concatenate([row[:, None] * freqs, col[:, None] * freqs], axis=-1)
    cos = jnp.repeat(jnp.cos(ang), 2, axis=-1)
    sin = jnp.repeat(jnp.sin(ang), 2, axis=-1)
    sign = jnp.where(jnp.arange(HEAD_DIM) % 2 == 0, -1.0, 1.0).astype(F32)
    reps = LANES // HEAD_DIM
    return jnp.tile(cos, (1, reps)), jnp.tile(sin * sign, (1, reps))


def _trunk(x, mods, layers, big):
    l = x.shape[1]
    cos, sin = _rope_tables(l)
    for i, (mod, p) in enumerate(zip(mods, layers)):
        x = _ffn(x, mod, p["norm_ffn1_g"], big["ffn1_w_gu"], big["ffn1_w_down"], sub=0, layer=i)
        qa, ka, va, qb, kt, vx, u = _inproj(x, mod, p["norm_mix_g"], big["w_in"], p["gqa"], p["gka"],
                                            p["gqb"], p["gkb"], cos, sin, layer=i)
        ya = _na_attention(qa, ka, va, big["na_bias"], layer=i)
        yb = _gqa_attention(qb, kt, vx)
        yc = _conv(u, p["conv_w"], p["conv_b"], p["conv_ln_g"], p["conv_ln_b"])
        x = _outproj_ffn(x, ya, yb, yc, mod, p["out_norm_g"], big["w_out"], p["norm_ffn2_g"],
                         big["ffn2_w_gu"], big["ffn2_w_down"], layer=i)
    return x


def kernel(x_prompt, x_sample, c_prompt, c_sample, w_mod, b_mod, norm_ffn1_g, ffn1_w_gu, ffn1_w_down,
           norm_mix_g, w_in, na_q_g, na_k_g, na_rpb, gq_q_g, gq_k_g, conv_w, conv_b, conv_ln_g, conv_ln_b,
           out_norm_g, w_out, norm_ffn2_g, ffn2_w_gu, ffn2_w_down):
    depth, d = norm_ffn1_g.shape
    bp = c_prompt.shape[0]
    bs = c_sample.shape[0]
    pad = (-(bp + bs)) % 8
    c_all = jnp.concatenate([c_prompt, c_sample, jnp.zeros((pad, d), F32)], axis=0)
    mod_all = _mod_all(c_all, w_mod, b_mod).reshape(depth, bp + bs + pad, N_MOD, d)

    tile_gain = lambda g, w: jnp.tile(g.astype(F32), w // HEAD_DIM).reshape(1, w)
    big = dict(ffn1_w_gu=ffn1_w_gu.astype(BF16), ffn1_w_down=ffn1_w_down.astype(BF16), w_in=w_in.astype(BF16),
               w_out=w_out.astype(BF16), ffn2_w_gu=ffn2_w_gu.astype(BF16), ffn2_w_down=ffn2_w_down.astype(BF16),
               na_bias=_na_bias(na_rpb))
    layers = []
    for i in range(depth):
        layers.append(dict(
            norm_ffn1_g=norm_ffn1_g[i], norm_mix_g=norm_mix_g[i],
            gqa=tile_gain(na_q_g[i], NA_W), gka=tile_gain(na_k_g[i], NA_W),
            gqb=tile_gain(gq_q_g[i], 2 * LANES), gkb=tile_gain(gq_k_g[i], KV_W),
            conv_w=conv_w[i], conv_b=conv_b[i], conv_ln_g=conv_ln_g[i],
            conv_ln_b=conv_ln_b[i], out_norm_g=out_norm_g[i], norm_ffn2_g=norm_ffn2_g[i]))
    y_prompt = _trunk(x_prompt, [mod_all[i, :bp] for i in range(depth)], layers, big)
    y_sample = _trunk(x_sample, [mod_all[i, bp:bp + bs] for i in range(depth)], layers, big)
    return (y_prompt, y_sample)
```

```python
import functools

import jax
import jax.numpy as jnp
from jax import lax
from jax.experimental import pallas as pl
from jax.experimental.pallas import tpu as pltpu

F32 = jnp.float32
BF16 = jnp.bfloat16

GRID_W = 64
HEAD_DIM = 64
NA_HEADS = 4
NA_WIN_ROWS = 8
NA_WIN_COLS = 16
GQ_HEADS = 8
GQ_KV_HEADS = 2
GQ_GROUP = GQ_HEADS // GQ_KV_HEADS
CONV_CH = 256
CONV_WIDTH = 31
CONV_PAD = CONV_WIDTH // 2
ROPE_THETA = 10000.0
EPS = 1e-6
NEG_INF = -1e30
LOG2E = 1.4426950408889634
NA_W = NA_HEADS * HEAD_DIM
GQ_W = GQ_HEADS * HEAD_DIM
KV_W = GQ_KV_HEADS * HEAD_DIM
N_MOD = 9
RPB_ROWS = 2 * NA_WIN_ROWS - 1
RPB_COLS = 2 * NA_WIN_COLS - 1

LANES = 128
SUBLANES = 8
MXU_WIDTH = 256
NA_BLOCK_ROWS = 4
NA_BLOCK = NA_BLOCK_ROWS * GRID_W
NA_SLOTS = 3 * NA_BLOCK_ROWS
NA_BLOCKS_PER_STEP = 8
TOKEN_TILE = 512
FFN_SUBTILES = 2
GQA_QUERY_TILE = 512
GQA_KEY_CHUNK = 1024
GQA_TILES_PER_STEP = 4
CONV_ROW_CHUNK = 64
CONV_HALO = 16
VMEM_LIMIT = 56 * 1024 * 1024


def _tile(n, target, align=8):
    if n <= target:
        return n
    for t in range(target - target % align, 0, -align):
        if n % t == 0:
            return t
    return n


def _token_block(l):
    tm = _tile(l, FFN_SUBTILES * TOKEN_TILE, TOKEN_TILE)
    assert tm % min(TOKEN_TILE, tm) == 0, "sequence length must split into row sub-tiles"
    return tm


def _params(sem):
    return pltpu.CompilerParams(dimension_semantics=sem, vmem_limit_bytes=VMEM_LIMIT)


def _sigmoid(x):
    return 1.0 / (1.0 + jnp.exp(-x))


def _mod_norm(x, g, shift, scale):
    ms = jnp.mean(x * x, axis=-1, keepdims=True)
    y = x * lax.rsqrt(ms + EPS) * g
    return y * (1.0 + scale) + shift


def _mod_kernel(c_ref, w_ref, b_ref, o_ref):
    c = c_ref[...]
    ca = (c * _sigmoid(c)).astype(BF16)
    o_ref[0] = jnp.dot(ca, w_ref[0].astype(BF16), preferred_element_type=F32) + b_ref[0]


def _mod_all(c, w_mod, b_mod):
    depth, d, nd = w_mod.shape
    bp = c.shape[0]
    tn = _tile(nd, 1024, LANES)
    return pl.pallas_call(
        _mod_kernel,
        out_shape=jax.ShapeDtypeStruct((depth, bp, nd), F32),
        grid=(depth, nd // tn),
        in_specs=[
            pl.BlockSpec((bp, d), lambda l, n: (0, 0)),
            pl.BlockSpec((1, d, tn), lambda l, n: (l, 0, n)),
            pl.BlockSpec((1, 1, tn), lambda l, n: (l, 0, n)),
        ],
        out_specs=pl.BlockSpec((1, bp, tn), lambda l, n: (l, 0, n)),
        compiler_params=_params(("arbitrary", "arbitrary")),
        name="mod_vectors",
    )(c, w_mod, b_mod.reshape(depth, 1, nd))


def _swiglu_half_step(x, mod_ref, g_ref, wgu_ref, wd_ref, sub):
    dff = wd_ref.shape[0]
    h = _mod_norm(x, g_ref[...], mod_ref[0, 3 * sub:3 * sub + 1, :],
                  mod_ref[0, 3 * sub + 1:3 * sub + 2, :]).astype(BF16)
    parts = []
    for c0 in range(0, dff, MXU_WIDTH):
        cw = min(MXU_WIDTH, dff - c0)
        gate = jnp.dot(h, wgu_ref[:, c0:c0 + cw], preferred_element_type=F32)
        up = jnp.dot(h, wgu_ref[:, dff + c0:dff + c0 + cw], preferred_element_type=F32)
        parts.append((gate * _sigmoid(gate) * up).astype(BF16))
    a = jnp.concatenate(parts, axis=1)
    y = jnp.dot(a, wd_ref[...], preferred_element_type=F32)
    return x + (0.5 * mod_ref[0, 3 * sub + 2:3 * sub + 3, :]) * y


def _ffn_kernel(x_ref, mod_ref, g_ref, wgu_ref, wd_ref, o_ref, *, sub):
    step = min(TOKEN_TILE, x_ref.shape[1])
    for r0 in range(0, x_ref.shape[1], step):
        rows = slice(r0, r0 + step)
        o_ref[0, rows, :] = _swiglu_half_step(x_ref[0, rows, :], mod_ref, g_ref, wgu_ref, wd_ref, sub)


def _layer_weight(w, layer):
    return pl.BlockSpec((None,) + w.shape[1:], lambda bi, i: (layer, 0, 0), pipeline_mode=pl.Buffered(1))


def _ffn(x, mod, g, w_gu, w_down, *, sub, layer):
    b, l, d = x.shape
    tm = _token_block(l)
    return pl.pallas_call(
        functools.partial(_ffn_kernel, sub=sub),
        out_shape=jax.ShapeDtypeStruct(x.shape, F32),
        grid=(b, l // tm),
        in_specs=[
            pl.BlockSpec((1, tm, d), lambda bi, i: (bi, i, 0)),
            pl.BlockSpec((1, N_MOD, d), lambda bi, i: (bi, 0, 0)),
            pl.BlockSpec((1, d), lambda bi, i: (0, 0)),
            _layer_weight(w_gu, layer),
            _layer_weight(w_down, layer),
        ],
        out_specs=pl.BlockSpec((1, tm, d), lambda bi, i: (bi, i, 0)),
        compiler_params=_params(("parallel", "parallel")),
        name=f"ffn{sub}",
    )(x, mod, g.reshape(1, d), w_gu, w_down)


def _head_rms(z, gain, seg):
    sq = z * z
    hi = sq.astype(BF16)
    lo = (sq - hi.astype(F32)).astype(BF16)
    ssq = jnp.dot(hi, seg, preferred_element_type=F32) + jnp.dot(lo, seg, preferred_element_type=F32)
    return z * lax.rsqrt(ssq * (1.0 / HEAD_DIM) + EPS) * gain


def _rope(x, cos, sin_signed):
    lane = lax.broadcasted_iota(jnp.int32, x.shape, 1)
    partner = jnp.where(lane % 2 == 0, pltpu.roll(x, LANES - 1, 1), pltpu.roll(x, 1, 1))
    return x * cos + partner * sin_signed


def _inproj_kernel(x_ref, mod_ref, g_ref, w_ref, gqa_ref, gka_ref, gqb_ref, gkb_ref, cos_ref, sin_ref,
                   qa_ref, ka_ref, va_ref, qb_ref, kt_ref, vx_ref, u_ref):
    r = lax.broadcasted_iota(jnp.int32, (2 * LANES, 2 * LANES), 0) // HEAD_DIM
    c = lax.broadcasted_iota(jnp.int32, (2 * LANES, 2 * LANES), 1) // HEAD_DIM
    seg = (r == c).astype(BF16)
    scale = HEAD_DIM ** -0.5 * LOG2E
    step = min(TOKEN_TILE, x_ref.shape[1])
    for r0 in range(0, x_ref.shape[1], step):
        rows = slice(r0, r0 + step)
        h = _mod_norm(x_ref[0, rows, :], g_ref[...], mod_ref[0, 3:4, :], mod_ref[0, 4:5, :]).astype(BF16)
        z = jnp.dot(h, w_ref[...], preferred_element_type=F32)
        cos = cos_ref[rows, :]
        sin = sin_ref[rows, :]

        o = 0
        qa_ref[0, rows, :] = (_head_rms(z[:, o:o + NA_W], gqa_ref[...], seg) * scale).astype(BF16)
        o += NA_W
        ka_ref[0, rows, :] = _head_rms(z[:, o:o + NA_W], gka_ref[...], seg).astype(BF16)
        o += NA_W
        va_ref[0, rows, :] = z[:, o:o + NA_W].astype(BF16)
        o += NA_W
        for cix in range(GQ_W // (2 * LANES)):
            qn = _head_rms(z[:, o:o + 2 * LANES], gqb_ref[...], seg)
            for half in range(2):
                qr = _rope(qn[:, half * LANES:(half + 1) * LANES], cos, sin)
                lo_ = cix * 2 * LANES + half * LANES
                qb_ref[0, rows, lo_:lo_ + LANES] = (qr * scale).astype(BF16)
            o += 2 * LANES
        kn = _head_rms(z[:, o:o + KV_W], gkb_ref[...], seg[:KV_W, :KV_W])
        kr = _rope(kn, cos, sin)
        kt_ref[0, :, :, rows] = kr.T.astype(BF16).reshape(GQ_KV_HEADS, HEAD_DIM, step)
        o += KV_W
        vb = z[:, o:o + KV_W]
        lane = lax.broadcasted_iota(jnp.int32, vb.shape, 1)
        ones_col = jnp.where(lane == HEAD_DIM, 1.0, 0.0)
        vx_ref[0, 0, rows, :] = jnp.where(lane < HEAD_DIM, vb, ones_col).astype(BF16)
        vx_ref[0, 1, rows, :] = jnp.where(lane < HEAD_DIM, pltpu.roll(vb, HEAD_DIM, 1), ones_col).astype(BF16)
        o += KV_W
        ca = z[:, o:o + CONV_CH]
        cb = z[:, o + CONV_CH:o + 2 * CONV_CH]
        u_ref[0, rows, :] = ca * _sigmoid(cb)


def _inproj(x, mod, g, w_in, gqa, gka, gqb, gkb, cos, sin, *, layer):
    b, l, d = x.shape
    tm = _token_block(l)
    tok = lambda w: pl.BlockSpec((1, tm, w), lambda bi, i: (bi, i, 0))
    vec = lambda w: pl.BlockSpec((1, w), lambda bi, i: (0, 0))
    return pl.pallas_call(
        _inproj_kernel,
        out_shape=(
            jax.ShapeDtypeStruct((b, l, NA_W), BF16),
            jax.ShapeDtypeStruct((b, l, NA_W), BF16),
            jax.ShapeDtypeStruct((b, l, NA_W), BF16),
            jax.ShapeDtypeStruct((b, l, GQ_W), BF16),
            jax.ShapeDtypeStruct((b, GQ_KV_HEADS, HEAD_DIM, l), BF16),
            jax.ShapeDtypeStruct((b, GQ_KV_HEADS, l, LANES), BF16),
            jax.ShapeDtypeStruct((b, l, CONV_CH), F32),
        ),
        grid=(b, l // tm),
        in_specs=[
            tok(d),
            pl.BlockSpec((1, N_MOD, d), lambda bi, i: (bi, 0, 0)),
            vec(d),
            _layer_weight(w_in, layer),
            vec(NA_W), vec(NA_W), vec(2 * LANES), vec(KV_W),
            pl.BlockSpec((tm, LANES), lambda bi, i: (i, 0)),
            pl.BlockSpec((tm, LANES), lambda bi, i: (i, 0)),
        ],
        out_specs=(
            tok(NA_W), tok(NA_W), tok(NA_W), tok(GQ_W),
            pl.BlockSpec((1, GQ_KV_HEADS, HEAD_DIM, tm), lambda bi, i: (bi, 0, 0, i)),
            pl.BlockSpec((1, GQ_KV_HEADS, tm, LANES), lambda bi, i: (bi, 0, i, 0)),
            tok(CONV_CH),
        ),
        compiler_params=_params(("parallel", "parallel")),
        name="inproj",
    )(x, mod, g.reshape(1, d), w_in, gqa, gka, gqb, gkb, cos, sin)


def _na_kernel(q_ref, kp_ref, kc_ref, kn_ref, vp_ref, vc_ref, vn_ref, bias_ref, o_ref, *, n, g):
    i = pl.program_id(1)
    nt = (((1,), (1,)), ((), ()))
    lane = lax.broadcasted_iota(jnp.int32, (NA_BLOCK, LANES), 1)
    for u in range(g):
        sb = g * i + u
        case = jnp.where(sb == 0, 0, jnp.where(sb == n - 1, 2, 1))
        if u == 0:
            parts = [(kp_ref, vp_ref, 0, NA_BLOCK), (kc_ref, vc_ref, 0, 2 * NA_BLOCK)]
        elif u == g - 1:
            parts = [(kc_ref, vc_ref, (g - 2) * NA_BLOCK, 2 * NA_BLOCK), (kn_ref, vn_ref, 0, NA_BLOCK)]
        else:
            parts = [(kc_ref, vc_ref, (u - 1) * NA_BLOCK, 3 * NA_BLOCK)]
        rows = slice(u * NA_BLOCK, (u + 1) * NA_BLOCK)
        for hp in range(NA_HEADS // 2):
            cols = slice(hp * LANES, (hp + 1) * LANES)
            q2 = q_ref[0, rows, cols]
            zero = jnp.zeros_like(q2)
            qm = jnp.concatenate([jnp.where(lane < HEAD_DIM, q2, zero), jnp.where(lane < HEAD_DIM, zero, q2)], axis=0)
            bias = bias_ref[case, 2 * hp:2 * hp + 2].reshape(2 * NA_BLOCK, NA_SLOTS * GRID_W)
            scores, col = [], 0
            for k_ref, _, r0, nr in parts:
                scores.append(lax.dot_general(qm, k_ref[0, r0:r0 + nr, cols], nt, preferred_element_type=F32)
                              + bias[:, col:col + nr])
                col += nr
            m = functools.reduce(jnp.maximum, [jnp.max(s, axis=-1, keepdims=True) for s in scores])
            probs = [jnp.exp2(s - m) for s in scores]
            den = sum(jnp.sum(p, axis=-1, keepdims=True) for p in probs)
            o = sum(jnp.dot(p.astype(BF16), v_ref[0, r0:r0 + nr, cols], preferred_element_type=F32)
                    for p, (_, v_ref, r0, nr) in zip(probs, parts))
            o = o / den
            o_ref[0, rows, cols] = jnp.where(lane < HEAD_DIM, o[:NA_BLOCK], o[NA_BLOCK:])


def _na_bias(rpb):
    depth = rpb.shape[0]
    c = jnp.arange(GRID_W)
    cs = jnp.clip(c - NA_WIN_COLS // 2, 0, GRID_W - NA_WIN_COLS)
    col_off = jnp.clip(c[None, :] - c[:, None] + (NA_WIN_COLS - 1), 0, RPB_COLS - 1)
    col_ok = (c[None, :] >= cs[:, None]) & (c[None, :] < cs[:, None] + NA_WIN_COLS)
    pick_col = (col_off[None] == jnp.arange(RPB_COLS)[:, None, None]).astype(F32)
    a = jnp.arange(NA_BLOCK_ROWS)[:, None]
    s = jnp.arange(NA_SLOTS)[None, :]
    rel = s - NA_BLOCK_ROWS - a
    half = NA_WIN_ROWS // 2
    valid = jnp.stack([
        jnp.broadcast_to((s >= NA_BLOCK_ROWS) & (s < NA_BLOCK_ROWS + NA_WIN_ROWS), rel.shape),
        (rel >= -half) & (rel < NA_WIN_ROWS - half),
        jnp.broadcast_to(s < NA_WIN_ROWS, rel.shape),
    ])
    ro = jnp.clip(rel + (NA_WIN_ROWS - 1), 0, RPB_ROWS - 1)
    pick_row = (ro[..., None] == jnp.arange(RPB_ROWS)).astype(F32)
    t = jnp.einsum("lhrk,kqw,asr->lhaqsw", rpb.astype(F32) * LOG2E, pick_col, pick_row,
                   precision=lax.Precision.HIGHEST)
    ok = valid[:, :, None, :, None] & col_ok[None, None, :, None, :]
    t = jnp.where(ok[None, :, None], t[:, None], NEG_INF)
    return t.reshape(depth, 3, NA_HEADS, NA_BLOCK, NA_SLOTS * GRID_W)


def _na_attention(qa, ka, va, bias, *, layer):
    b, l, w = qa.shape
    n = l // NA_BLOCK
    g = _tile(n, NA_BLOCKS_PER_STEP, 1)
    assert l % NA_BLOCK == 0 and g >= 2 and n >= 3, "sequence must split into steps of at least two query blocks"
    cur = pl.BlockSpec((1, g * NA_BLOCK, w), lambda bi, i: (bi, i, 0))
    prev = pl.BlockSpec((1, NA_BLOCK, w), lambda bi, i: (bi, jnp.maximum(g * i - 1, 0), 0))
    nxt = pl.BlockSpec((1, NA_BLOCK, w), lambda bi, i: (bi, jnp.minimum(g * (i + 1), n - 1), 0))
    return pl.pallas_call(
        functools.partial(_na_kernel, n=n, g=g),
        out_shape=jax.ShapeDtypeStruct((b, l, w), F32),
        grid=(b, n // g),
        in_specs=[cur, prev, cur, nxt, prev, cur, nxt,
                  pl.BlockSpec((None,) + bias.shape[1:], lambda bi, i: (layer, 0, 0, 0, 0),
                               pipeline_mode=pl.Buffered(1))],
        out_specs=cur,
        compiler_params=_params(("parallel", "parallel")),
        name="na_attention",
    )(qa, ka, ka, ka, va, va, va, bias)


def _gqa_kernel(q_ref, kt_ref, v_ref, o_ref, qs_ref, s_ref, m_ref, acc_ref, *, tq, tk, nk, tiles):
    def row0(i):
        return i * tq if isinstance(i, int) else pl.multiple_of(i * tq, tq)

    def stack(i):
        for h in range(GQ_GROUP):
            qs_ref[i % 2, h * tq:(h + 1) * tq, :] = q_ref[0, pl.ds(row0(i), tq), h * HEAD_DIM:(h + 1) * HEAD_DIM]

    def scores(i, j, slot):
        k0 = pl.multiple_of(j * tk, tk)
        s_ref[slot] = jnp.dot(qs_ref[i % 2], kt_ref[0, 0, :, pl.ds(k0, tk)], preferred_element_type=F32)

    def accumulate(j, slot):
        k0 = pl.multiple_of(j * tk, tk)
        s = s_ref[slot]
        m_prev = m_ref[...]
        m_new = jnp.maximum(m_prev, jnp.max(s, axis=-1, keepdims=True))
        alpha = jnp.exp2(m_prev - m_new)
        p = jnp.exp2(s - jnp.tile(m_new, (1, tk // LANES)))
        pv = jnp.dot(p.astype(BF16), v_ref[0, 0, pl.ds(k0, tk), :], preferred_element_type=F32)
        acc_ref[...] = alpha * acc_ref[...] + pv
        m_ref[...] = m_new

    def finalize(i):
        for h in range(GQ_GROUP):
            a = acc_ref[h * tq:(h + 1) * tq, :]
            o_ref[0, pl.ds(row0(i), tq), h * HEAD_DIM:(h + 1) * HEAD_DIM] = (
                a[:, :HEAD_DIM] / a[:, HEAD_DIM:HEAD_DIM + 1])

    def tile(i, has_next):
        m_ref[...] = jnp.full(m_ref.shape, NEG_INF, F32)
        acc_ref[...] = jnp.zeros_like(acc_ref)

        def pair(jj, carry):
            j = 2 * jj
            scores(i, j + 1, 1)
            accumulate(j, 0)
            scores(i, j + 2, 0)
            accumulate(j + 1, 1)
            return carry

        lax.fori_loop(0, nk // 2 - 1, pair, 0)
        scores(i, nk - 1, 1)
        accumulate(nk - 2, 0)
        if has_next:
            stack(i + 1)
            scores(i + 1, 0, 0)
        accumulate(nk - 1, 1)
        finalize(i)

    stack(0)
    scores(0, 0, 0)

    def tile_with_next(i, carry):
        tile(i, True)
        return carry

    lax.fori_loop(0, tiles - 1, tile_with_next, 0)
    tile(tiles - 1, False)


def _gqa_attention(qb, kt, vx):
    b, l, w = qb.shape
    tq = _tile(l, GQA_QUERY_TILE)
    tk = _tile(l // 2, GQA_KEY_CHUNK, LANES)
    nk = l // tk
    assert nk % 2 == 0, "key chunks are consumed in pairs"
    tiles = _tile(l // tq, GQA_TILES_PER_STEP, 1)
    gw = GQ_GROUP * HEAD_DIM
    rows = GQ_GROUP * tq
    return pl.pallas_call(
        functools.partial(_gqa_kernel, tq=tq, tk=tk, nk=nk, tiles=tiles),
        out_shape=jax.ShapeDtypeStruct((b, l, w), F32),
        grid=(b, GQ_KV_HEADS, l // (tq * tiles)),
        in_specs=[
            pl.BlockSpec((1, tq * tiles, gw), lambda bi, kv, i: (bi, i, kv)),
            pl.BlockSpec((1, 1, HEAD_DIM, l), lambda bi, kv, i: (bi, kv, 0, 0), pipeline_mode=pl.Buffered(1)),
            pl.BlockSpec((1, 1, l, LANES), lambda bi, kv, i: (bi, kv, 0, 0), pipeline_mode=pl.Buffered(1)),
        ],
        out_specs=pl.BlockSpec((1, tq * tiles, gw), lambda bi, kv, i: (bi, i, kv)),
        scratch_shapes=[pltpu.VMEM((2, rows, HEAD_DIM), BF16), pltpu.VMEM((2, rows, tk), F32),
                        pltpu.VMEM((rows, LANES), F32), pltpu.VMEM((rows, LANES), F32)],
        compiler_params=_params(("parallel", "parallel", "arbitrary")),
        name="gqa_attention",
    )(qb, kt, vx)


def _conv_kernel(u_ref, up_ref, un_ref, w_ref, b_ref, g_ref, beta_ref, o_ref, ext_ref, *, t, nt, rc):
    i = pl.program_id(1)
    ext_ref[0, 0:CONV_HALO, :] = jnp.where(i > 0, up_ref[0], 0.0)
    ext_ref[0, CONV_HALO:CONV_HALO + t, :] = u_ref[0]
    ext_ref[0, CONV_HALO + t:, :] = jnp.where(i < nt - 1, un_ref[0], 0.0)
    ext = ext_ref[0]
    for r in range(1, SUBLANES):
        ext_ref[r] = pltpu.roll(ext, ext.shape[0] - r, 0)
    base = CONV_HALO - CONV_PAD
    for c0 in range(0, t, rc):
        acc = jnp.zeros((rc, CONV_CH), F32)
        for k in range(CONV_WIDTH):
            r = (base + k) % SUBLANES
            a0 = c0 + base + k - r
            acc = acc + ext_ref[r, a0:a0 + rc, :] * w_ref[k:k + 1, :]
        y = acc + b_ref[...]
        mu = jnp.mean(y, axis=-1, keepdims=True)
        yc = y - mu
        var = jnp.mean(yc * yc, axis=-1, keepdims=True)
        yn = yc * lax.rsqrt(var + EPS) * g_ref[...] + beta_ref[...]
        o_ref[0, c0:c0 + rc, :] = yn * _sigmoid(yn)


def _conv(u, w, bias, ln_g, ln_b):
    b, l, ch = u.shape
    t = _tile(l, TOKEN_TILE)
    nt = l // t
    rc = _tile(t, CONV_ROW_CHUNK)
    hb = t // CONV_HALO
    nh = l // CONV_HALO
    vec = pl.BlockSpec((1, ch), lambda bi, i: (0, 0))
    return pl.pallas_call(
        functools.partial(_conv_kernel, t=t, nt=nt, rc=rc),
        out_shape=jax.ShapeDtypeStruct(u.shape, F32),
        grid=(b, nt),
        in_specs=[
            pl.BlockSpec((1, t, ch), lambda bi, i: (bi, i, 0)),
            pl.BlockSpec((1, CONV_HALO, ch), lambda bi, i: (bi, jnp.maximum(i * hb - 1, 0), 0)),
            pl.BlockSpec((1, CONV_HALO, ch), lambda bi, i: (bi, jnp.minimum((i + 1) * hb, nh - 1), 0)),
            pl.BlockSpec((CONV_WIDTH, ch), lambda bi, i: (0, 0)),
            vec, vec, vec,
        ],
        out_specs=pl.BlockSpec((1, t, ch), lambda bi, i: (bi, i, 0)),
        scratch_shapes=[pltpu.VMEM((SUBLANES, t + 2 * CONV_HALO, ch), F32)],
        compiler_params=_params(("parallel", "parallel")),
        name="conformer_conv",
    )(u, u, u, w, bias.reshape(1, ch), ln_g.reshape(1, ch), ln_b.reshape(1, ch))


def _group_rms(y, g):
    ms = jnp.mean(y * y, axis=-1, keepdims=True)
    return (y * lax.rsqrt(ms + EPS) * g).astype(BF16)


def _outproj_ffn_kernel(x_ref, ya_ref, yb_ref, yc_ref, mod_ref, og_ref, wo_ref, g_ref, wgu_ref, wd_ref, o_ref):
    step = min(TOKEN_TILE, x_ref.shape[1])
    for r0 in range(0, x_ref.shape[1], step):
        rows = slice(r0, r0 + step)
        y = jnp.dot(_group_rms(ya_ref[0, rows, :], og_ref[:, :NA_W]), wo_ref[:NA_W, :],
                    preferred_element_type=F32)
        y += jnp.dot(_group_rms(yb_ref[0, rows, :], og_ref[:, NA_W:NA_W + GQ_W]), wo_ref[NA_W:NA_W + GQ_W, :],
                     preferred_element_type=F32)
        y += jnp.dot(_group_rms(yc_ref[0, rows, :], og_ref[:, NA_W + GQ_W:]), wo_ref[NA_W + GQ_W:, :],
                     preferred_element_type=F32)
        x = x_ref[0, rows, :] + mod_ref[0, 5:6, :] * y
        o_ref[0, rows, :] = _swiglu_half_step(x, mod_ref, g_ref, wgu_ref, wd_ref, 2)


def _outproj_ffn(x, ya, yb, yc, mod, og, w_out, g, w_gu, w_down, *, layer):
    b, l, d = x.shape
    mix = w_out.shape[1]
    tm = _token_block(l)
    tok = lambda w: pl.BlockSpec((1, tm, w), lambda bi, i: (bi, i, 0))
    return pl.pallas_call(
        _outproj_ffn_kernel,
        out_shape=jax.ShapeDtypeStruct(x.shape, F32),
        grid=(b, l // tm),
        in_specs=[
            tok(d), tok(NA_W), tok(GQ_W), tok(CONV_CH),
            pl.BlockSpec((1, N_MOD, d), lambda bi, i: (bi, 0, 0)),
            pl.BlockSpec((1, mix), lambda bi, i: (0, 0)),
            _layer_weight(w_out, layer),
            pl.BlockSpec((1, d), lambda bi, i: (0, 0)),
            _layer_weight(w_gu, layer),
            _layer_weight(w_down, layer),
        ],
        out_specs=tok(d),
        compiler_params=_params(("parallel", "parallel")),
        name="outproj_ffn2",
    )(x, ya, yb, yc, mod, og.reshape(1, mix), w_out, g.reshape(1, d), w_gu, w_down)


def _rope_tables(l):
    t = jnp.arange(l)
    row = (t // GRID_W).astype(F32)
    col = (t % GRID_W).astype(F32)
    half = HEAD_DIM // 2
    freqs = ROPE_THETA ** (-jnp.arange(0, half, 2, dtype=F32) / half)
    ang = jnp.concatenate([row[:, None] * freqs, col[:, None] * freqs], axis=-1)
    cos = jnp.repeat(jnp.cos(ang), 2, axis=-1)
    sin = jnp.repeat(jnp.sin(ang), 2, axis=-1)
    sign = jnp.where(jnp.arange(HEAD_DIM) % 2 == 0, -1.0, 1.0).astype(F32)
    reps = LANES // HEAD_DIM
    return jnp.tile(cos, (1, reps)), jnp.tile(sin * sign, (1, reps))


def _trunk(x, mods, layers, big):
    l = x.shape[1]
    cos, sin = _rope_tables(l)
    for i, (mod, p) in enumerate(zip(mods, layers)):
        x = _ffn(x, mod, p["norm_ffn1_g"], big["ffn1_w_gu"], big["ffn1_w_down"], sub=0, layer=i)
        qa, ka, va, qb, kt, vx, u = _inproj(x, mod, p["norm_mix_g"], big["w_in"], p["gqa"], p["gka"],
                                            p["gqb"], p["gkb"], cos, sin, layer=i)
        ya = _na_attention(qa, ka, va, big["na_bias"], layer=i)
        yb = _gqa_attention(qb, kt, vx)
        yc = _conv(u, p["conv_w"], p["conv_b"], p["conv_ln_g"], p["conv_ln_b"])
        x = _outproj_ffn(x, ya, yb, yc, mod, p["out_norm_g"], big["w_out"], p["norm_ffn2_g"],
                         big["ffn2_w_gu"], big["ffn2_w_down"], layer=i)
    return x


def kernel(x_prompt, x_sample, c_prompt, c_sample, w_mod, b_mod, norm_ffn1_g, ffn1_w_gu, ffn1_w_down,
           norm_mix_g, w_in, na_q_g, na_k_g, na_rpb, gq_q_g, gq_k_g, conv_w, conv_b, conv_ln_g, conv_ln_b,
           out_norm_g, w_out, norm_ffn2_g, ffn2_w_gu, ffn2_w_down):
    depth, d = norm_ffn1_g.shape
    bp = c_prompt.shape[0]
    bs = c_sample.shape[0]
    pad = (-(bp + bs)) % 8
    c_all = jnp.concatenate([c_prompt, c_sample, jnp.zeros((pad, d), F32)], axis=0)
    mod_all = _mod_all(c_all, w_mod, b_mod).reshape(depth, bp + bs + pad, N_MOD, d)

    tile_gain = lambda g, w: jnp.tile(g.astype(F32), w // HEAD_DIM).reshape(1, w)
    big = dict(ffn1_w_gu=ffn1_w_gu.astype(BF16), ffn1_w_down=ffn1_w_down.astype(BF16), w_in=w_in.astype(BF16),
               w_out=w_out.astype(BF16), ffn2_w_gu=ffn2_w_gu.astype(BF16), ffn2_w_down=ffn2_w_down.astype(BF16),
               na_bias=_na_bias(na_rpb))
    layers = []
    for i in range(depth):
        layers.append(dict(
            norm_ffn1_g=norm_ffn1_g[i], norm_mix_g=norm_mix_g[i],
            gqa=tile_gain(na_q_g[i], NA_W), gka=tile_gain(na_k_g[i], NA_W),
            gqb=tile_gain(gq_q_g[i], 2 * LANES), gkb=tile_gain(gq_k_g[i], KV_W),
            conv_w=conv_w[i], conv_b=conv_b[i], conv_ln_g=conv_ln_g[i],
            conv_ln_b=conv_ln_b[i], out_norm_g=out_norm_g[i], norm_ffn2_g=norm_ffn2_g[i]))
    y_prompt = _trunk(x_prompt, [mod_all[i, :bp] for i in range(depth)], layers, big)
    y_sample = _trunk(x_sample, [mod_all[i, bp:bp + bs] for i in range(depth)], layers, big)
    return (y_prompt, y_sample)
```
